```python
import math
import jax, jax.numpy as jnp
from jax import lax
import numpy as np

D_MODEL = 1024
BATCH = 8
SEQ = 4096
DEPTH = 1

D_PLE = 256
D_MIX = D_MODEL
ATTN_HEADS = 8
HEAD_DIM = 64
D_ATTN = ATTN_HEADS * HEAD_DIM
POOL_WINDOWS = (2, 4, 8, 16)
POOL_GROUPS = len(POOL_WINDOWS)
D_POOL = D_MIX - D_ATTN
POOL_CH = D_POOL // POOL_GROUPS
D_IN = 3 * D_ATTN + ATTN_HEADS + D_POOL
D_FF = int(math.ceil(8 * D_MODEL / 3 / 256) * 256)
Q_BLOCK = 128
RMS_EPS = 1e-6

kernel_name = "hymba_fox_poolformer_block"


def rms_norm(x, g):
    xf = x.astype(jnp.float32)
    y = xf * lax.rsqrt(jnp.mean(xf * xf, axis=-1, keepdims=True) + RMS_EPS)
    return (y * g.astype(jnp.float32)).astype(x.dtype)


def forgetting_attention(q, k, v, log_f):
    b, s, h, dh = q.shape
    scale = 1.0 / math.sqrt(dh)
    c = jnp.cumsum(log_f, axis=1).transpose(0, 2, 1)
    nb = s // Q_BLOCK
    qb = q.reshape(b, nb, Q_BLOCK, h, dh).transpose(1, 0, 2, 3, 4)
    cb = c.reshape(b, h, nb, Q_BLOCK).transpose(2, 0, 1, 3)
    starts = jnp.arange(nb, dtype=jnp.int32) * Q_BLOCK
    kpos = jnp.arange(s, dtype=jnp.int32)

    def one_block(args):
        q_i, c_i, s0 = args
        scores = jnp.einsum('bqhd,bkhd->bhqk', q_i, k).astype(jnp.float32) * scale
        scores = scores + c_i[:, :, :, None] - c[:, :, None, :]
        qpos = s0 + jnp.arange(Q_BLOCK, dtype=jnp.int32)
        causal = kpos[None, :] <= qpos[:, None]
        scores = jnp.where(causal, scores, -jnp.inf)
        probs = jax.nn.softmax(scores, axis=-1).astype(v.dtype)
        return jnp.einsum('bhqk,bkhd->bqhd', probs, v)

    out = lax.map(one_block, (qb, cb, starts))
    return out.transpose(1, 0, 2, 3, 4).reshape(b, s, h * dh)


def multiscale_pool(u, w_pool, pool_scale):
    b, s, _ = u.shape
    uf = u.astype(jnp.float32)
    cs = jnp.cumsum(uf, axis=1)
    pos = jnp.arange(s, dtype=jnp.int32)
    outs = []
    for g, w in enumerate(POOL_WINDOWS):
        lo, hi = g * POOL_CH, (g + 1) * POOL_CH
        cs_g = cs[:, :, lo:hi]
        cs_shift = jnp.pad(cs_g, ((0, 0), (w, 0), (0, 0)))[:, :s]
        count = jnp.minimum(pos + 1, w).astype(jnp.float32)[None, :, None]
        outs.append((cs_g - cs_shift) / count - uf[:, :, lo:hi])
    y = jnp.stack(outs, axis=2).astype(u.dtype)
    y = jnp.einsum('bsgc,gcd->bsgd', y, w_pool).reshape(b, s, D_POOL)
    return y * pool_scale


def setup_inputs(seed: int = 0) -> dict:
    key = jax.random.key(seed)
    ks = jax.random.split(key, 24)
    f32 = jnp.float32

    def nrm(k, shape, fan_in):
        return jax.random.normal(k, shape, f32) * (fan_in ** -0.5)

    def gain(k, shape):
        return 1.0 + 0.05 * jax.random.normal(k, shape, f32)

    return {
        "x": jax.random.normal(ks[0], (BATCH, SEQ, D_MODEL), f32),
        "p": jax.random.normal(ks[1], (DEPTH, BATCH, SEQ, D_PLE), f32),
        "g_mix_pre": gain(ks[2], (DEPTH, D_MODEL)),
        "w_in": nrm(ks[3], (DEPTH, D_MODEL, D_IN), D_MODEL),
        "b_forget": jax.random.uniform(ks[4], (DEPTH, ATTN_HEADS), f32, 1.0, 4.0),
        "g_attn_grp": gain(ks[5], (DEPTH, D_ATTN)),
        "g_pool_grp": gain(ks[6], (DEPTH, D_POOL)),
        "w_pool": nrm(ks[7], (DEPTH, POOL_GROUPS, POOL_CH, POOL_CH), POOL_CH),
        "pool_scale": 1.0 + 0.1 * jax.random.normal(ks[8], (DEPTH, D_POOL), f32),
        "w_out": nrm(ks[9], (DEPTH, D_MIX, D_MODEL), D_MIX),
        "g_mix_post": gain(ks[10], (DEPTH, D_MODEL)),
        "g_ffn_pre": gain(ks[11], (DEPTH, D_MODEL)),
        "w_ffn_gate": nrm(ks[12], (DEPTH, D_MODEL, D_FF), D_MODEL),
        "w_ffn_up": nrm(ks[13], (DEPTH, D_MODEL, D_FF), D_MODEL),
        "w_ffn_down": nrm(ks[14], (DEPTH, D_FF, D_MODEL), D_FF),
        "g_ffn_post": gain(ks[15], (DEPTH, D_MODEL)),
        "w_ple_proj": nrm(ks[16], (DEPTH, D_PLE, D_MODEL), D_PLE),
        "g_ple": gain(ks[17], (DEPTH, D_MODEL)),
        "w_ple_gate": nrm(ks[18], (DEPTH, D_MODEL, D_MODEL), D_MODEL),
    }


def reference(x, p, g_mix_pre, w_in, b_forget, g_attn_grp, g_pool_grp, w_pool, pool_scale,
              w_out, g_mix_post, g_ffn_pre, w_ffn_gate, w_ffn_up, w_ffn_down, g_ffn_post,
              w_ple_proj, g_ple, w_ple_gate):
    b, s, _ = x.shape
    h = x
    for i in range(DEPTH):
        hn = rms_norm(h, g_mix_pre[i])
        z = hn @ w_in[i]
        o = 0
        q = z[..., o:o + D_ATTN].reshape(b, s, ATTN_HEADS, HEAD_DIM); o += D_ATTN
        k = z[..., o:o + D_ATTN].reshape(b, s, ATTN_HEADS, HEAD_DIM); o += D_ATTN
        v = z[..., o:o + D_ATTN].reshape(b, s, ATTN_HEADS, HEAD_DIM); o += D_ATTN
        f_logit = z[..., o:o + ATTN_HEADS]; o += ATTN_HEADS
        u = z[..., o:o + D_POOL]
        log_f = jax.nn.log_sigmoid(f_logit.astype(jnp.float32) + b_forget[i].astype(jnp.float32))
        a = forgetting_attention(q, k, v, log_f)
        m = multiscale_pool(u, w_pool[i], pool_scale[i])
        mix = jnp.concatenate([rms_norm(a, g_attn_grp[i]), rms_norm(m, g_pool_grp[i])], axis=-1)
        h = h + rms_norm(mix @ w_out[i], g_mix_post[i])
        hn = rms_norm(h, g_ffn_pre[i])
        ff = (jax.nn.silu(hn @ w_ffn_gate[i]) * (hn @ w_ffn_up[i])) @ w_ffn_down[i]
        h = h + rms_norm(ff, g_ffn_post[i])
        e = rms_norm(p[i] @ w_ple_proj[i], g_ple[i])
        h = h + jax.nn.sigmoid(h @ w_ple_gate[i]) * e
    return h
```

```python
import functools
import math

import jax
import jax.numpy as jnp
import numpy as np
from jax import lax
from jax.experimental import pallas as pl
from jax.experimental.pallas import tpu as pltpu

D_MODEL = 1024
D_PLE = 256
ATTN_HEADS = 8
HEAD_DIM = 64
D_ATTN = ATTN_HEADS * HEAD_DIM
POOL_WINDOWS = (2, 4, 8, 16)
POOL_GROUPS = len(POOL_WINDOWS)
POOL_CH = 128
D_POOL = POOL_GROUPS * POOL_CH
D_FF = 2816
RMS_EPS = 1e-6

LANES = 128
HALO = 16
CUMSUM_CHUNK = 256
N_PIECES = 3
W1_COLS = 3 * D_ATTN + D_POOL + LANES

PROJ_ROWS = 512
ATTN_Q_ROWS = 512
ATTN_K_ROWS = 512
TAIL_ROWS = 512
FF_CHUNK = 256
VMEM_LIMIT_BYTES = 56 * 1024 * 1024

BF16 = jnp.bfloat16
F32 = jnp.float32


def _aug_lane(head):
    return HEAD_DIM if head % 2 == 0 else 0


def _rms(x, g):
    ms = jnp.mean(x * x, axis=-1, keepdims=True)
    return x * lax.rsqrt(ms + RMS_EPS) * g


def _split3(x):
    hi = x.astype(BF16)
    r1 = x - hi.astype(F32)
    mid = r1.astype(BF16)
    lo = (r1 - mid.astype(F32)).astype(BF16)
    return hi, mid, lo


def _proj_kernel(x_ref, g_ref, w1_ref, bf_ref, sel_ref, wpool_ref, pscale_ref, gpool_ref,
                 q_ref, k_ref, v_ref, nm_ref, carry_ref, halo_ref):
    i = pl.program_id(1)
    tm = x_ref.shape[1]

    @pl.when(i == 0)
    def _():
        carry_ref[...] = jnp.zeros_like(carry_ref)
        halo_ref[...] = jnp.zeros_like(halo_ref)

    hn = _rms(x_ref[0], g_ref[...]).astype(BF16)
    z = jnp.dot(hn, w1_ref[...], preferred_element_type=F32)

    lane = lax.broadcasted_iota(jnp.int32, (tm, LANES), 1)

    zf = z[:, 3 * D_ATTN + D_POOL:] + bf_ref[...]
    logf = jnp.minimum(zf, 0.0) - jnp.log1p(jnp.exp(-jnp.abs(zf)))
    logf = jnp.where(lane < N_PIECES * ATTN_HEADS, logf, 0.0)
    pieces = _split3(logf)
    r = lax.broadcasted_iota(jnp.int32, (CUMSUM_CHUNK, CUMSUM_CHUNK), 0)
    c = lax.broadcasted_iota(jnp.int32, (CUMSUM_CHUNK, CUMSUM_CHUNK), 1)
    tri = jnp.where(c <= r, 1.0, 0.0).astype(BF16)
    carry = carry_ref[0:1, :]
    chunks = []
    for r0 in range(0, tm, CUMSUM_CHUNK):
        cs = carry
        for piece in pieces:
            cs = cs + jnp.dot(tri, piece[r0:r0 + CUMSUM_CHUNK], preferred_element_type=F32)
        carry = cs[CUMSUM_CHUNK - 1:CUMSUM_CHUNK, :]
        chunks.append(cs)
    carry_ref[0:1, :] = carry
    negc = -jnp.concatenate(chunks, axis=0)
    nhi, nmid, nlo = _split3(negc)
    cp = jnp.where(lane < ATTN_HEADS, nhi.astype(F32),
                   jnp.where(lane < 2 * ATTN_HEADS, nmid.astype(F32), nlo.astype(F32))).astype(BF16)

    for h in range(ATTN_HEADS):
        slab = slice((h // 2) * LANES, (h // 2 + 1) * LANES)
        a0 = _aug_lane(h)
        data = (lane < HEAD_DIM) if h % 2 == 0 else (lane >= HEAD_DIM)
        ones3 = jnp.where((lane >= a0) & (lane < a0 + N_PIECES), 1.0, 0.0)
        one1 = jnp.where(lane == a0, 1.0, 0.0)
        kaug = jnp.dot(cp, sel_ref[h], preferred_element_type=F32)
        q_ref[0, h] = jnp.where(data, z[:, 0 * D_ATTN:1 * D_ATTN][:, slab], ones3).astype(BF16)
        k_ref[0, h] = jnp.where(data, z[:, 1 * D_ATTN:2 * D_ATTN][:, slab], kaug).astype(BF16)
        v_ref[0, h] = jnp.where(data, z[:, 2 * D_ATTN:3 * D_ATTN][:, slab], one1).astype(BF16)

    u = z[:, 3 * D_ATTN:3 * D_ATTN + D_POOL]
    ue = jnp.concatenate([halo_ref[...], u], axis=0)
    halo_ref[...] = u[tm - HALO:, :]
    pos = i * tm + lax.broadcasted_iota(jnp.int32, (tm, POOL_CH), 0)
    ms = []
    for g, w in enumerate(POOL_WINDOWS):
        ch = slice(g * POOL_CH, (g + 1) * POOL_CH)
        s = ue[:, ch]
        shift = 1
        while shift < w:
            s = s + pltpu.roll(s, shift, axis=0)
            shift *= 2
        inv_count = 1.0 / jnp.minimum(pos + 1, w).astype(F32)
        y = s[HALO:] * inv_count - u[:, ch]
        m = jnp.dot(y.astype(BF16), wpool_ref[g], preferred_element_type=F32)
        ms.append(m * pscale_ref[:, ch])
    msq = sum(jnp.sum(m * m, axis=-1, keepdims=True) for m in ms) * (1.0 / D_POOL)
    rstd = lax.rsqrt(msq + RMS_EPS)
    for g in range(POOL_GROUPS):
        ch = slice(g * POOL_CH, (g + 1) * POOL_CH)
        nm_ref[0, :, ch] = (ms[g] * rstd * gpool_ref[:, ch]).astype(BF16)


def _proj_call(x, g_pre, w1, bf3, sel, wpool, pscale, gpool):
    b, s, _ = x.shape
    tm = PROJ_ROWS
    const = lambda *shape: pl.BlockSpec(shape, lambda bi, i: (0,) * len(shape))
    head_spec = pl.BlockSpec((1, ATTN_HEADS, tm, LANES), lambda bi, i: (bi, 0, i, 0))
    head_shape = jax.ShapeDtypeStruct((b, ATTN_HEADS, s, LANES), BF16)
    return pl.pallas_call(
        _proj_kernel,
        grid=(b, s // tm),
        in_specs=[
            pl.BlockSpec((1, tm, D_MODEL), lambda bi, i: (bi, i, 0)),
            const(1, D_MODEL),
            const(D_MODEL, W1_COLS),
            const(1, LANES),
            const(ATTN_HEADS, LANES, LANES),
            const(POOL_GROUPS, POOL_CH, POOL_CH),
            const(1, D_POOL),
            const(1, D_POOL),
        ],
        out_specs=[head_spec, head_spec, head_spec,
                   pl.BlockSpec((1, tm, D_POOL), lambda bi, i: (bi, i, 0))],
        out_shape=[head_shape, head_shape, head_shape,
                   jax.ShapeDtypeStruct((b, s, D_POOL), BF16)],
        scratch_shapes=[pltpu.VMEM((8, LANES), F32), pltpu.VMEM((HALO, D_POOL), F32)],
        compiler_params=pltpu.CompilerParams(
            dimension_semantics=("arbitrary", "arbitrary"),
            vmem_limit_bytes=VMEM_LIMIT_BYTES),
        name="proj",
    )(x, g_pre, w1, bf3, sel, wpool, pscale, gpool)


def _attn_kernel(q_ref, k_ref, v_ref, o_ref):
    i = pl.program_id(2)
    tq = q_ref.shape[2]
    tk = ATTN_K_ROWS
    nt = (((1,), (1,)), ((), ()))
    lane = lax.broadcasted_iota(jnp.int32, (tq, LANES), 1)

    outs = []
    for hh in range(2):
        q = q_ref[0, hh]

        def step(k, v, m, acc, mask):
            s = lax.dot_general(q, k, nt, preferred_element_type=F32)
            if mask is not None:
                s = jnp.where(mask, s, -jnp.inf)
            m_new = jnp.maximum(m, jnp.max(s, axis=-1, keepdims=True))
            p = jnp.exp(s - m_new)
            alpha = jnp.exp(m - m_new)
            acc = alpha * acc + jnp.dot(p.astype(BF16), v, preferred_element_type=F32)
            return m_new, acc

        def body(j, carry):
            m, acc = carry
            rows = pl.ds(pl.multiple_of(j * tk, tk), tk)
            return step(k_ref[0, hh, rows, :], v_ref[0, hh, rows, :], m, acc, None)

        m0 = jnp.full((tq, 1), -jnp.inf, F32)
        acc0 = jnp.zeros((tq, LANES), F32)
        m, acc = lax.fori_loop(0, i * (tq // tk), body, (m0, acc0))
        rows = pl.ds(pl.multiple_of(i * tq, tq), tq)
        r = lax.broadcasted_iota(jnp.int32, (tq, tq), 0)
        c = lax.broadcasted_iota(jnp.int32, (tq, tq), 1)
        m, acc = step(k_ref[0, hh, rows, :], v_ref[0, hh, rows, :], m, acc, c <= r)
        l_lane = _aug_lane(hh)
        outs.append(acc / acc[:, l_lane:l_lane + 1])
    o_ref[0] = jnp.where(lane < HEAD_DIM, outs[0], outs[1])


def _attn_call(q, k, v):
    b, h, s, _ = q.shape
    tq = ATTN_Q_ROWS
    kv_spec = pl.BlockSpec((1, 2, s, LANES), lambda bi, hp, i: (bi, hp, 0, 0))
    return pl.pallas_call(
        _attn_kernel,
        grid=(b, h // 2, s // tq),
        in_specs=[pl.BlockSpec((1, 2, tq, LANES), lambda bi, hp, i: (bi, hp, i, 0)),
                  kv_spec, kv_spec],
        out_specs=pl.BlockSpec((1, tq, LANES), lambda bi, hp, i: (bi, i, hp)),
        out_shape=jax.ShapeDtypeStruct((b, s, D_ATTN), F32),
        compiler_params=pltpu.CompilerParams(
            dimension_semantics=("arbitrary", "arbitrary", "arbitrary"),
            vmem_limit_bytes=VMEM_LIMIT_BYTES),
        name="fox_attn",
    )(q, k, v)


def _tail_kernel(x_ref, a_ref, nm_ref, p_ref, gattn_ref, wo_ref, gpost_ref, gffn_ref,
                 wg_ref, wu_ref, wd_ref, gffpost_ref, wpp_ref, gple_ref, wpg_ref, o_ref):
    na = _rms(a_ref[...], gattn_ref[...]).astype(BF16)
    mix = (jnp.dot(na, wo_ref[:D_ATTN, :], preferred_element_type=F32)
           + jnp.dot(nm_ref[...], wo_ref[D_ATTN:, :], preferred_element_type=F32))
    h = x_ref[...] + _rms(mix, gpost_ref[...])

    hn = _rms(h, gffn_ref[...]).astype(BF16)
    ff = jnp.zeros(h.shape, F32)
    for c0 in range(0, D_FF, FF_CHUNK):
        cols = slice(c0, c0 + FF_CHUNK)
        gate = jnp.dot(hn, wg_ref[:, cols], preferred_element_type=F32)
        up = jnp.dot(hn, wu_ref[:, cols], preferred_element_type=F32)
        act = (gate * jax.nn.sigmoid(gate) * up).astype(BF16)
        ff = ff + jnp.dot(act, wd_ref[cols, :], preferred_element_type=F32)
    h = h + _rms(ff, gffpost_ref[...])

    e = _rms(jnp.dot(p_ref[...].astype(BF16), wpp_ref[...], preferred_element_type=F32),
             gple_ref[...])
    gate = jax.nn.sigmoid(jnp.dot(h.astype(BF16), wpg_ref[...], preferred_element_type=F32))
    o_ref[...] = h + gate * e


def _tail_call(x2, a2, nm2, p2, gattn, wo, gpost, gffn, wg, wu, wd, gffpost, wpp, gple, wpg):
    t = x2.shape[0]
    tm = TAIL_ROWS
    rows = lambda width: pl.BlockSpec((tm, width), lambda i: (i, 0))
    const = lambda *shape: pl.BlockSpec(shape, lambda i: (0,) * len(shape),
                                        pipeline_mode=pl.Buffered(1))
    return pl.pallas_call(
        _tail_kernel,
        grid=(t // tm,),
        in_specs=[
            rows(D_MODEL), rows(D_ATTN), rows(D_POOL), rows(D_PLE),
            const(1, D_ATTN), const(D_MODEL, D_MODEL), const(1, D_MODEL), const(1, D_MODEL),
            const(D_MODEL, D_FF), const(D_MODEL, D_FF), const(D_FF, D_MODEL), const(1, D_MODEL),
            const(D_PLE, D_MODEL), const(1, D_MODEL), const(D_MODEL, D_MODEL),
        ],
        out_specs=rows(D_MODEL),
        out_shape=jax.ShapeDtypeStruct((t, D_MODEL), F32),
        compiler_params=pltpu.CompilerParams(
            dimension_semantics=("arbitrary",),
            vmem_limit_bytes=VMEM_LIMIT_BYTES),
        name="tail",
    )(x2, a2, nm2, p2, gattn, wo, gpost, gffn, wg, wu, wd, gffpost, wpp, gple, wpg)


def _selection_matrices():
    sel = np.zeros((ATTN_HEADS, LANES, LANES), np.float32)
    for h in range(ATTN_HEADS):
        for r in range(N_PIECES):
            sel[h, r * ATTN_HEADS + h, _aug_lane(h) + r] = 1.0
    return jnp.asarray(sel, BF16)


def kernel(x, p, g_mix_pre, w_in, b_forget, g_attn_grp, g_pool_grp, w_pool, pool_scale, w_out,
           g_mix_post, g_ffn_pre, w_ffn_gate, w_ffn_up, w_ffn_down, g_ffn_post, w_ple_proj,
           g_ple, w_ple_gate):
    b, s, _ = x.shape
    depth = w_in.shape[0]
    assert s % PROJ_ROWS == 0 and s % ATTN_Q_ROWS == 0 and (b * s) % TAIL_ROWS == 0
    sel = _selection_matrices()
    row = lambda v: v.reshape(1, -1).astype(F32)
    h = x
    for i in range(depth):
        w = w_in[i]
        o_f = 3 * D_ATTN
        o_u = o_f + ATTN_HEADS
        wf = w[:, o_f:o_u]
        w1 = jnp.concatenate(
            [w[:, :D_ATTN] * (1.0 / math.sqrt(HEAD_DIM)), w[:, D_ATTN:o_f], w[:, o_u:],
             wf, wf, wf, jnp.zeros((D_MODEL, LANES - N_PIECES * ATTN_HEADS), F32)],
            axis=1).astype(BF16)
        bf = b_forget[i].astype(F32)
        bf3 = jnp.concatenate([bf, bf, bf, jnp.zeros((LANES - N_PIECES * ATTN_HEADS,), F32)])
        q, k, v, nm = _proj_call(h, row(g_mix_pre[i]), w1, bf3.reshape(1, LANES), sel,
                                 w_pool[i].astype(BF16), row(pool_scale[i]), row(g_pool_grp[i]))
        a = _attn_call(q, k, v)
        t = b * s
        h = _tail_call(
            h.reshape(t, D_MODEL), a.reshape(t, D_ATTN), nm.reshape(t, D_POOL),
            p[i].reshape(t, D_PLE), row(g_attn_grp[i]), w_out[i].astype(BF16),
            row(g_mix_post[i]), row(g_ffn_pre[i]), w_ffn_gate[i].astype(BF16),
            w_ffn_up[i].astype(BF16), w_ffn_down[i].astype(BF16), row(g_ffn_post[i]),
            w_ple_proj[i].astype(BF16), row(g_ple[i]), w_ple_gate[i].astype(BF16),
        ).reshape(b, s, D_MODEL)
    return h
```

```python
import functools
import math

import jax
import jax.numpy as jnp
import numpy as np
from jax import lax
from jax.experimental import pallas as pl
from jax.experimental.pallas import tpu as pltpu

D_MODEL = 1024
D_PLE = 256
ATTN_HEADS = 8
HEAD_DIM = 64
D_ATTN = ATTN_HEADS * HEAD_DIM
POOL_WINDOWS = (2, 4, 8, 16)
POOL_GROUPS = len(POOL_WINDOWS)
POOL_CH = 128
D_POOL = POOL_GROUPS * POOL_CH
D_FF = 2816
RMS_EPS = 1e-6
LOG2E = math.log2(math.e)

LANES = 128
HALO = 16
CUMSUM_CHUNK = 256
N_PIECES = 3
W1_COLS = 3 * D_ATTN + D_POOL + LANES

PROJ_ROWS = 512
ATTN_Q_ROWS = 512
ATTN_K_ROWS = 512
ATTN_HEADS_PER_STEP = 8
TAIL_ROWS = 512
FF_CHUNK = 256
VMEM_LIMIT_BYTES = 56 * 1024 * 1024

BF16 = jnp.bfloat16
F32 = jnp.float32


def _aug_lane(head):
    return HEAD_DIM if head % 2 == 0 else 0


def _rms(x, g):
    ms = jnp.mean(x * x, axis=-1, keepdims=True)
    return x * lax.rsqrt(ms + RMS_EPS) * g


def _split3(x):
    hi = x.astype(BF16)
    r1 = x - hi.astype(F32)
    mid = r1.astype(BF16)
    lo = (r1 - mid.astype(F32)).astype(BF16)
    return hi, mid, lo


def _proj_kernel(x_ref, g_ref, w1_ref, bf_ref, sel_ref, wpool_ref, pscale_ref, gpool_ref,
                 q_ref, k_ref, v_ref, nm_ref, carry_ref, halo_ref):
    i = pl.program_id(1)
    tm = x_ref.shape[1]

    @pl.when(i == 0)
    def _():
        carry_ref[...] = jnp.zeros_like(carry_ref)
        halo_ref[...] = jnp.zeros_like(halo_ref)

    hn = _rms(x_ref[0], g_ref[...]).astype(BF16)
    z = jnp.dot(hn, w1_ref[...], preferred_element_type=F32)

    lane = lax.broadcasted_iota(jnp.int32, (tm, LANES), 1)

    zf = z[:, 3 * D_ATTN + D_POOL:] + bf_ref[...]
    logf = jnp.minimum(zf, 0.0) - jnp.log1p(jnp.exp(-jnp.abs(zf)))
    logf = jnp.where(lane < N_PIECES * ATTN_HEADS, logf, 0.0)
    pieces = _split3(logf)
    r = lax.broadcasted_iota(jnp.int32, (CUMSUM_CHUNK, CUMSUM_CHUNK), 0)
    c = lax.broadcasted_iota(jnp.int32, (CUMSUM_CHUNK, CUMSUM_CHUNK), 1)
    tri = jnp.where(c <= r, 1.0, 0.0).astype(BF16)
    carry = carry_ref[0:1, :]
    chunks = []
    for r0 in range(0, tm, CUMSUM_CHUNK):
        cs = carry
        for piece in pieces:
            cs = cs + jnp.dot(tri, piece[r0:r0 + CUMSUM_CHUNK], preferred_element_type=F32)
        carry = cs[CUMSUM_CHUNK - 1:CUMSUM_CHUNK, :]
        chunks.append(cs)
    carry_ref[0:1, :] = carry
    negc = -LOG2E * jnp.concatenate(chunks, axis=0)
    nhi, nmid, nlo = _split3(negc)
    cp = jnp.where(lane < ATTN_HEADS, nhi.astype(F32),
                   jnp.where(lane < 2 * ATTN_HEADS, nmid.astype(F32), nlo.astype(F32))).astype(BF16)

    for h in range(ATTN_HEADS):
        slab = slice((h // 2) * LANES, (h // 2 + 1) * LANES)
        a0 = _aug_lane(h)
        data = (lane < HEAD_DIM) if h % 2 == 0 else (lane >= HEAD_DIM)
        ones3 = jnp.where((lane >= a0) & (lane < a0 + N_PIECES), 1.0, 0.0)
        one1 = jnp.where(lane == a0, 1.0, 0.0)
        kaug = jnp.dot(cp, sel_ref[h], preferred_element_type=F32)
        q_ref[0, h] = jnp.where(data, LOG2E * z[:, 0 * D_ATTN:1 * D_ATTN][:, slab],
                                ones3).astype(BF16)
        k_ref[0, h] = jnp.where(data, z[:, 1 * D_ATTN:2 * D_ATTN][:, slab], kaug).astype(BF16)
        v_ref[0, h] = jnp.where(data, z[:, 2 * D_ATTN:3 * D_ATTN][:, slab], one1).astype(BF16)

    u = z[:, 3 * D_ATTN:3 * D_ATTN + D_POOL]
    ue = jnp.concatenate([halo_ref[...], u], axis=0)
    halo_ref[...] = u[tm - HALO:, :]
    pos = i * tm + lax.broadcasted_iota(jnp.int32, (tm, POOL_CH), 0)
    ms = []
    for g, w in enumerate(POOL_WINDOWS):
        ch = slice(g * POOL_CH, (g + 1) * POOL_CH)
        s = ue[:, ch]
        shift = 1
        while shift < w:
            s = s + pltpu.roll(s, shift, axis=0)
            shift *= 2
        inv_count = 1.0 / jnp.minimum(pos + 1, w).astype(F32)
        y = s[HALO:] * inv_count - u[:, ch]
        m = jnp.dot(y.astype(BF16), wpool_ref[g], preferred_element_type=F32)
        ms.append(m * pscale_ref[:, ch])
    msq = sum(jnp.sum(m * m, axis=-1, keepdims=True) for m in ms) * (1.0 / D_POOL)
    rstd = lax.rsqrt(msq + RMS_EPS)
    for g in range(POOL_GROUPS):
        ch = slice(g * POOL_CH, (g + 1) * POOL_CH)
        nm_ref[0, :, ch] = (ms[g] * rstd * gpool_ref[:, ch]).astype(BF16)


def _proj_call(x, g_pre, w1, bf3, sel, wpool, pscale, gpool):
    b, s, _ = x.shape
    tm = PROJ_ROWS
    const = lambda *shape: pl.BlockSpec(shape, lambda bi, i: (0,) * len(shape))
    head_spec = pl.BlockSpec((1, ATTN_HEADS, tm, LANES), lambda bi, i: (bi, 0, i, 0))
    head_shape = jax.ShapeDtypeStruct((b, ATTN_HEADS, s, LANES), BF16)
    return pl.pallas_call(
        _proj_kernel,
        grid=(b, s // tm),
        in_specs=[
            pl.BlockSpec((1, tm, D_MODEL), lambda bi, i: (bi, i, 0)),
            const(1, D_MODEL),
            const(D_MODEL, W1_COLS),
            const(1, LANES),
            const(ATTN_HEADS, LANES, LANES),
            const(POOL_GROUPS, POOL_CH, POOL_CH),
            const(1, D_POOL),
            const(1, D_POOL),
        ],
        out_specs=[head_spec, head_spec, head_spec,
                   pl.BlockSpec((1, tm, D_POOL), lambda bi, i: (bi, i, 0))],
        out_shape=[head_shape, head_shape, head_shape,
                   jax.ShapeDtypeStruct((b, s, D_POOL), BF16)],
        scratch_shapes=[pltpu.VMEM((8, LANES), F32), pltpu.VMEM((HALO, D_POOL), F32)],
        compiler_params=pltpu.CompilerParams(
            dimension_semantics=("arbitrary", "arbitrary"),
            vmem_limit_bytes=VMEM_LIMIT_BYTES),
        name="proj",
    )(x, g_pre, w1, bf3, sel, wpool, pscale, gpool)


def _attn_kernel(q_ref, k_ref, v_ref, o_ref, *state_refs):
    i = pl.program_id(2)
    nh = q_ref.shape[1]
    tq = q_ref.shape[2]
    tk = ATTN_K_ROWS
    nt = (((1,), (1,)), ((), ()))
    lane = lax.broadcasted_iota(jnp.int32, (tq, LANES), 1)
    m_ref, acc_ref = state_refs[:nh], state_refs[nh:]

    for hh in range(nh):
        m_ref[hh][...] = jnp.full((tq, LANES), -jnp.inf, F32)
        acc_ref[hh][...] = jnp.zeros((tq, LANES), F32)

    def step(rows, mask):
        def qk(hh):
            return lax.dot_general(q_ref[0, hh], k_ref[0, hh, rows, :], nt,
                                   preferred_element_type=F32)

        s_next = qk(0)
        for hh in range(nh):
            s = s_next
            if hh + 1 < nh:
                s_next = qk(hh + 1)
            if mask is not None:
                s = jnp.where(mask, s, -jnp.inf)
            m = m_ref[hh][...]
            m_new = jnp.maximum(m, jnp.max(s, axis=-1, keepdims=True))
            p = jnp.exp2(s - pltpu.repeat(m_new, s.shape[1] // LANES, axis=1))
            alpha = jnp.exp2(m - m_new)
            acc_ref[hh][...] = alpha * acc_ref[hh][...] + jnp.dot(
                p.astype(BF16), v_ref[0, hh, rows, :], preferred_element_type=F32)
            m_ref[hh][...] = m_new

    @pl.loop(0, i * (tq // tk))
    def _(j):
        step(pl.ds(pl.multiple_of(j * tk, tk), tk), None)

    r = lax.broadcasted_iota(jnp.int32, (tq, tq), 0)
    c = lax.broadcasted_iota(jnp.int32, (tq, tq), 1)
    step(pl.ds(pl.multiple_of(i * tq, tq), tq), c <= r)
    for pair in range(nh // 2):
        outs = []
        for hh in (2 * pair, 2 * pair + 1):
            acc = acc_ref[hh][...]
            l_lane = _aug_lane(hh)
            outs.append(acc / acc[:, l_lane:l_lane + 1])
        o_ref[0, :, pair * LANES:(pair + 1) * LANES] = jnp.where(lane < HEAD_DIM, outs[0], outs[1])


def _attn_call(q, k, v):
    b, h, s, _ = q.shape
    tq = ATTN_Q_ROWS
    nh = ATTN_HEADS_PER_STEP
    kv_spec = pl.BlockSpec((1, nh, s, LANES), lambda bi, hg, i: (bi, hg, 0, 0))
    return pl.pallas_call(
        _attn_kernel,
        grid=(b, h // nh, s // tq),
        in_specs=[pl.BlockSpec((1, nh, tq, LANES), lambda bi, hg, i: (bi, hg, i, 0)),
                  kv_spec, kv_spec],
        out_specs=pl.BlockSpec((1, tq, nh * HEAD_DIM), lambda bi, hg, i: (bi, i, hg)),
        out_shape=jax.ShapeDtypeStruct((b, s, D_ATTN), F32),
        scratch_shapes=[pltpu.VMEM((tq, LANES), F32)] * (2 * nh),
        compiler_params=pltpu.CompilerParams(
            dimension_semantics=("arbitrary", "arbitrary", "arbitrary"),
            vmem_limit_bytes=VMEM_LIMIT_BYTES),
        name="fox_attn",
    )(q, k, v)


def _tail_kernel(x_ref, a_ref, nm_ref, p_ref, gattn_ref, wo_ref, gpost_ref, gffn_ref,
                 wg_ref, wu_ref, wd_ref, gffpost_ref, wpp_ref, gple_ref, wpg_ref, o_ref):
    na = _rms(a_ref[...], gattn_ref[...]).astype(BF16)
    mix = (jnp.dot(na, wo_ref[:D_ATTN, :], preferred_element_type=F32)
           + jnp.dot(nm_ref[...], wo_ref[D_ATTN:, :], preferred_element_type=F32))
    h = x_ref[...] + _rms(mix, gpost_ref[...])

    hn = _rms(h, gffn_ref[...]).astype(BF16)
    ff = jnp.zeros(h.shape, F32)
    for c0 in range(0, D_FF, FF_CHUNK):
        cols = slice(c0, c0 + FF_CHUNK)
        gate = jnp.dot(hn, wg_ref[:, cols], preferred_element_type=F32)
        up = jnp.dot(hn, wu_ref[:, cols], preferred_element_type=F32)
        act = (gate * jax.nn.sigmoid(gate) * up).astype(BF16)
        ff = ff + jnp.dot(act, wd_ref[cols, :], preferred_element_type=F32)
    h = h + _rms(ff, gffpost_ref[...])

    e = _rms(jnp.dot(p_ref[...].astype(BF16), wpp_ref[...], preferred_element_type=F32),
             gple_ref[...])
    gate = jax.nn.sigmoid(jnp.dot(h.astype(BF16), wpg_ref[...], preferred_element_type=F32))
    o_ref[...] = h + gate * e


def _tail_call(x2, a2, nm2, p2, gattn, wo, gpost, gffn, wg, wu, wd, gffpost, wpp, gple, wpg):
    t = x2.shape[0]
    tm = TAIL_ROWS
    rows = lambda width: pl.BlockSpec((tm, width), lambda i: (i, 0))
    const = lambda *shape: pl.BlockSpec(shape, lambda i: (0,) * len(shape),
                                        pipeline_mode=pl.Buffered(1))
    return pl.pallas_call(
        _tail_kernel,
        grid=(t // tm,),
        in_specs=[
            rows(D_MODEL), rows(D_ATTN), rows(D_POOL), rows(D_PLE),
            const(1, D_ATTN), const(D_MODEL, D_MODEL), const(1, D_MODEL), const(1, D_MODEL),
            const(D_MODEL, D_FF), const(D_MODEL, D_FF), const(D_FF, D_MODEL), const(1, D_MODEL),
            const(D_PLE, D_MODEL), const(1, D_MODEL), const(D_MODEL, D_MODEL),
        ],
        out_specs=rows(D_MODEL),
        out_shape=jax.ShapeDtypeStruct((t, D_MODEL), F32),
        compiler_params=pltpu.CompilerParams(
            dimension_semantics=("arbitrary",),
            vmem_limit_bytes=VMEM_LIMIT_BYTES),
        name="tail",
    )(x2, a2, nm2, p2, gattn, wo, gpost, gffn, wg, wu, wd, gffpost, wpp, gple, wpg)


def _selection_matrices():
    sel = np.zeros((ATTN_HEADS, LANES, LANES), np.float32)
    for h in range(ATTN_HEADS):
        for r in range(N_PIECES):
            sel[h, r * ATTN_HEADS + h, _aug_lane(h) + r] = 1.0
    return jnp.asarray(sel, BF16)


def kernel(x, p, g_mix_pre, w_in, b_forget, g_attn_grp, g_pool_grp, w_pool, pool_scale, w_out,
           g_mix_post, g_ffn_pre, w_ffn_gate, w_ffn_up, w_ffn_down, g_ffn_post, w_ple_proj,
           g_ple, w_ple_gate):
    b, s, _ = x.shape
    depth = w_in.shape[0]
    assert s % PROJ_ROWS == 0 and s % ATTN_Q_ROWS == 0 and (b * s) % TAIL_ROWS == 0
    sel = _selection_matrices()
    row = lambda v: v.reshape(1, -1).astype(F32)
    h = x
    for i in range(depth):
        w = w_in[i]
        o_f = 3 * D_ATTN
        o_u = o_f + ATTN_HEADS
        wf = w[:, o_f:o_u]
        w1 = jnp.concatenate(
            [w[:, :D_ATTN] * (1.0 / math.sqrt(HEAD_DIM)), w[:, D_ATTN:o_f], w[:, o_u:],
             wf, wf, wf, jnp.zeros((D_MODEL, LANES - N_PIECES * ATTN_HEADS), F32)],
            axis=1).astype(BF16)
        bf = b_forget[i].astype(F32)
        bf3 = jnp.concatenate([bf, bf, bf, jnp.zeros((LANES - N_PIECES * ATTN_HEADS,), F32)])
        q, k, v, nm = _proj_call(h, row(g_mix_pre[i]), w1, bf3.reshape(1, LANES), sel,
                                 w_pool[i].astype(BF16), row(pool_scale[i]), row(g_pool_grp[i]))
        a = _attn_call(q, k, v)
        t = b * s
        h = _tail_call(
            h.reshape(t, D_MODEL), a.reshape(t, D_ATTN), nm.reshape(t, D_POOL),
            p[i].reshape(t, D_PLE), row(g_attn_grp[i]), w_out[i].astype(BF16),
            row(g_mix_post[i]), row(g_ffn_pre[i]), w_ffn_gate[i].astype(BF16),
            w_ffn_up[i].astype(BF16), w_ffn_down[i].astype(BF16), row(g_ffn_post[i]),
            w_ple_proj[i].astype(BF16), row(g_ple[i]), w_ple_gate[i].astype(BF16),
        ).reshape(b, s, D_MODEL)
    return h
```

```python
import functools
import math

import jax
import jax.numpy as jnp
import numpy as np
from jax import lax
from jax.experimental import pallas as pl
from jax.experimental.pallas import tpu as pltpu

D_MODEL = 1024
D_PLE = 256
ATTN_HEADS = 8
HEAD_DIM = 64
D_ATTN = ATTN_HEADS * HEAD_DIM
POOL_WINDOWS = (2, 4, 8, 16)
POOL_GROUPS = len(POOL_WINDOWS)
POOL_CH = 128
D_POOL = POOL_GROUPS * POOL_CH
D_FF = 2816
RMS_EPS = 1e-6
LOG2E = math.log2(math.e)

LANES = 128
HALO = 16
CUMSUM_CHUNK = 256
N_PIECES = 3
W1_COLS = 3 * D_ATTN + D_POOL + LANES

PROJ_ROWS = 512
ATTN_Q_ROWS = 512
ITEMS_PER_TRIP = 8
STATS_ROWS = 8
SKIP_LOG2 = 160.0
NORM_SLACK = 1.01
TAIL_ROWS = 512
FF_CHUNK = 256
VMEM_LIMIT_BYTES = 56 * 1024 * 1024

BF16 = jnp.bfloat16
F32 = jnp.float32


def _aug_lane(head):
    return HEAD_DIM if head % 2 == 0 else 0


def _rms(x, g):
    ms = jnp.mean(x * x, axis=-1, keepdims=True)
    return x * lax.rsqrt(ms + RMS_EPS) * g


def _split3(x):
    hi = x.astype(BF16)
    r1 = x - hi.astype(F32)
    mid = r1.astype(BF16)
    lo = (r1 - mid.astype(F32)).astype(BF16)
    return hi, mid, lo


def _proj_kernel(x_ref, g_ref, w1_ref, bf_ref, sel_ref, hsum_ref, wpool_ref, pscale_ref, gpool_ref,
                 q_ref, k_ref, v_ref, nm_ref, stats_ref, carry_ref, halo_ref):
    i = pl.program_id(1)
    tm = x_ref.shape[1]

    @pl.when(i == 0)
    def _():
        carry_ref[...] = jnp.zeros_like(carry_ref)
        halo_ref[...] = jnp.zeros_like(halo_ref)

    hn = _rms(x_ref[0], g_ref[...]).astype(BF16)
    z = jnp.dot(hn, w1_ref[...], preferred_element_type=F32)

    lane = lax.broadcasted_iota(jnp.int32, (tm, LANES), 1)

    zf = z[:, 3 * D_ATTN + D_POOL:] + bf_ref[...]
    logf = jnp.minimum(zf, 0.0) - jnp.log1p(jnp.exp(-jnp.abs(zf)))
    logf = jnp.where(lane < N_PIECES * ATTN_HEADS, logf, 0.0)
    pieces = _split3(logf)
    r = lax.broadcasted_iota(jnp.int32, (CUMSUM_CHUNK, CUMSUM_CHUNK), 0)
    c = lax.broadcasted_iota(jnp.int32, (CUMSUM_CHUNK, CUMSUM_CHUNK), 1)
    tri = jnp.where(c <= r, 1.0, 0.0).astype(BF16)
    carry = carry_ref[0:1, :]
    chunks = []
    for r0 in range(0, tm, CUMSUM_CHUNK):
        cs = carry
        for piece in pieces:
            cs = cs + jnp.dot(tri, piece[r0:r0 + CUMSUM_CHUNK], preferred_element_type=F32)
        carry = cs[CUMSUM_CHUNK - 1:CUMSUM_CHUNK, :]
        chunks.append(cs)
    carry_ref[0:1, :] = carry
    negc = -LOG2E * jnp.concatenate(chunks, axis=0)
    nhi, nmid, nlo = _split3(negc)
    cp = jnp.where(lane < ATTN_HEADS, nhi.astype(F32),
                   jnp.where(lane < 2 * ATTN_HEADS, nmid.astype(F32), nlo.astype(F32))).astype(BF16)
    kaug_all = jnp.dot(cp, sel_ref[...], preferred_element_type=F32)

    for h in range(ATTN_HEADS):
        slab = slice((h // 2) * LANES, (h // 2 + 1) * LANES)
        a0 = _aug_lane(h)
        data = (lane < HEAD_DIM) if h % 2 == 0 else (lane >= HEAD_DIM)
        ones3 = jnp.where((lane >= a0) & (lane < a0 + N_PIECES), 1.0, 0.0)
        one1 = jnp.where(lane == a0, 1.0, 0.0)
        kaug = kaug_all[:, h * LANES:(h + 1) * LANES]
        q_ref[0, h] = jnp.where(data, LOG2E * z[:, 0 * D_ATTN:1 * D_ATTN][:, slab],
                                ones3).astype(BF16)
        k_ref[0, h] = jnp.where(data, z[:, 1 * D_ATTN:2 * D_ATTN][:, slab], kaug).astype(BF16)
        v_ref[0, h] = jnp.where(data, z[:, 2 * D_ATTN:3 * D_ATTN][:, slab], one1).astype(BF16)

    zqk = z[:, :2 * D_ATTN]
    sqn = jnp.dot((zqk * zqk).astype(BF16), hsum_ref[...], preferred_element_type=F32)
    stats_ref[0, 0, 0:1, :] = negc[0:1, :]
    stats_ref[0, 0, 1:2, :] = negc[tm - 1:tm, :]
    stats_ref[0, 0, 2:3, :] = jnp.max(sqn, axis=0, keepdims=True)
    stats_ref[0, 0, 3:8, :] = jnp.zeros((5, LANES), F32)

    u = z[:, 3 * D_ATTN:3 * D_ATTN + D_POOL]
    ue = jnp.concatenate([halo_ref[...], u], axis=0)
    halo_ref[...] = u[tm - HALO:, :]
    pos = i * tm + lax.broadcasted_iota(jnp.int32, (tm, POOL_CH), 0)
    ms = []
    for g, w in enumerate(POOL_WINDOWS):
        ch = slice(g * POOL_CH, (g + 1) * POOL_CH)
        s = ue[:, ch]
        shift = 1
        while shift < w:
            s = s + pltpu.roll(s, shift, axis=0)
            shift *= 2
        inv_count = 1.0 / jnp.minimum(pos + 1, w).astype(F32)
        y = s[HALO:] * inv_count - u[:, ch]
        m = jnp.dot(y.astype(BF16), wpool_ref[g], preferred_element_type=F32)
        ms.append(m * pscale_ref[:, ch])
    msq = sum(jnp.sum(m * m, axis=-1, keepdims=True) for m in ms) * (1.0 / D_POOL)
    rstd = lax.rsqrt(msq + RMS_EPS)
    for g in range(POOL_GROUPS):
        ch = slice(g * POOL_CH, (g + 1) * POOL_CH)
        nm_ref[0, :, ch] = (ms[g] * rstd * gpool_ref[:, ch]).astype(BF16)


def _proj_call(x, g_pre, w1, bf3, sel, hsum, wpool, pscale, gpool):
    b, s, _ = x.shape
    tm = PROJ_ROWS
    const = lambda *shape: pl.BlockSpec(shape, lambda bi, i: (0,) * len(shape))
    head_spec = pl.BlockSpec((1, ATTN_HEADS, tm, LANES), lambda bi, i: (bi, 0, i, 0))
    head_shape = jax.ShapeDtypeStruct((b, ATTN_HEADS, s, LANES), BF16)
    return pl.pallas_call(
        _proj_kernel,
        grid=(b, s // tm),
        in_specs=[
            pl.BlockSpec((1, tm, D_MODEL), lambda bi, i: (bi, i, 0)),
            const(1, D_MODEL),
            const(D_MODEL, W1_COLS),
            const(1, LANES),
            const(LANES, ATTN_HEADS * LANES),
            const(2 * D_ATTN, LANES),
            const(POOL_GROUPS, POOL_CH, POOL_CH),
            const(1, D_POOL),
            const(1, D_POOL),
        ],
        out_specs=[head_spec, head_spec, head_spec,
                   pl.BlockSpec((1, tm, D_POOL), lambda bi, i: (bi, i, 0)),
                   pl.BlockSpec((1, 1, STATS_ROWS, LANES), lambda bi, i: (bi, i, 0, 0))],
        out_shape=[head_shape, head_shape, head_shape,
                   jax.ShapeDtypeStruct((b, s, D_POOL), BF16),
                   jax.ShapeDtypeStruct((b, s // tm, STATS_ROWS, LANES), F32)],
        scratch_shapes=[pltpu.VMEM((8, LANES), F32), pltpu.VMEM((HALO, D_POOL), F32)],
        compiler_params=pltpu.CompilerParams(
            dimension_semantics=("arbitrary", "arbitrary"),
            vmem_limit_bytes=VMEM_LIMIT_BYTES),
        name="proj",
    )(x, g_pre, w1, bf3, sel, hsum, wpool, pscale, gpool)


def _attn_kernel(items_ref, ntrips_ref, q_ref, k_ref, v_ref, o_ref, m_ref, acc_ref):
    step_id = pl.program_id(0) * pl.num_programs(1) + pl.program_id(1)
    i = pl.program_id(1)
    nh = q_ref.shape[1]
    tq = q_ref.shape[2]
    n_codes = pl.num_programs(1) * nh
    nt = (((1,), (1,)), ((), ()))
    lane = lax.broadcasted_iota(jnp.int32, (tq, LANES), 1)

    m_ref[...] = jnp.full(m_ref.shape, -jnp.inf, F32)
    acc_ref[...] = jnp.zeros(acc_ref.shape, F32)

    def qk(head, rows):
        return lax.dot_general(q_ref[0, head], k_ref[0, head, rows, :], nt,
                               preferred_element_type=F32)

    def run(heads, slots, rows, mask):
        s_next = qk(heads[0], rows[0])
        for n in range(len(heads)):
            s = s_next
            if n + 1 < len(heads):
                s_next = qk(heads[n + 1], rows[n + 1])
            if mask is not None:
                s = jnp.where(mask, s, -jnp.inf)
            m = m_ref[slots[n]]
            m_new = jnp.maximum(m, jnp.max(s, axis=-1, keepdims=True))
            p = jnp.exp2(s - jnp.concatenate([m_new] * (s.shape[1] // LANES), axis=1))
            alpha = jnp.exp2(m - m_new)
            acc_ref[slots[n]] = alpha * acc_ref[slots[n]] + jnp.dot(
                p.astype(BF16), v_ref[0, heads[n], rows[n], :], preferred_element_type=F32)
            m_ref[slots[n]] = m_new

    @pl.loop(0, ntrips_ref[step_id])
    def _(trip):
        heads, slots, rows = [], [], []
        for n in range(ITEMS_PER_TRIP):
            code = items_ref[step_id * n_codes + trip * ITEMS_PER_TRIP + n]
            valid = code < n_codes
            head = jnp.where(valid, code % nh, 0)
            block = jnp.where(valid, code // nh, 0)
            heads.append(head)
            slots.append(jnp.where(valid, head, nh))
            rows.append(pl.ds(pl.multiple_of(block * tq, tq), tq))
        run(heads, slots, rows, None)

    r = lax.broadcasted_iota(jnp.int32, (tq, tq), 0)
    c = lax.broadcasted_iota(jnp.int32, (tq, tq), 1)
    run(list(range(nh)), list(range(nh)), [pl.ds(pl.multiple_of(i * tq, tq), tq)] * nh, c <= r)
    for pair in range(nh // 2):
        outs = []
        for hh in (2 * pair, 2 * pair + 1):
            acc = acc_ref[hh]
            l_lane = _aug_lane(hh)
            outs.append(acc / acc[:, l_lane:l_lane + 1])
        o_ref[0, :, pair * LANES:(pair + 1) * LANES] = jnp.where(lane < HEAD_DIM, outs[0], outs[1])


def _attention_schedule(stats):
    nb = stats.shape[1]
    nh = ATTN_HEADS
    neg_first = stats[:, :, 0, :nh]
    neg_last = stats[:, :, 1, :nh]
    qn = jnp.sqrt(stats[:, :, 2, :nh]) * (LOG2E * NORM_SLACK)
    kn = jnp.sqrt(stats[:, :, 2, nh:2 * nh]) * NORM_SLACK
    top = neg_last[:, None, :, :] + qn[:, :, None, :] * kn[:, None, :, :]
    floor = neg_first[:, :, None, :] - (qn * kn)[:, :, None, :]
    tile_i = lax.broadcasted_iota(jnp.int32, (1, nb, nb, nh), 1)
    block_j = lax.broadcasted_iota(jnp.int32, (1, nb, nb, nh), 2)
    head = lax.broadcasted_iota(jnp.int32, (1, nb, nb, nh), 3)
    need = (block_j < tile_i) & ~(top - floor <= -SKIP_LOG2)
    n_codes = nb * nh
    codes = jnp.where(need, block_j * nh + head, n_codes).reshape(stats.shape[0], nb, n_codes)
    items = jnp.sort(codes, axis=-1).astype(jnp.int32)
    count = jnp.sum(need, axis=(2, 3)).astype(jnp.int32)
    ntrips = (count + ITEMS_PER_TRIP - 1) // ITEMS_PER_TRIP
    return items.reshape(-1), ntrips.reshape(-1)


def _attn_call(q, k, v, stats):
    b, nh, s, _ = q.shape
    tq = ATTN_Q_ROWS
    items, ntrips = _attention_schedule(stats)
    kv_spec = pl.BlockSpec((1, nh, s, LANES), lambda bi, i, *_: (bi, 0, 0, 0))
    return pl.pallas_call(
        _attn_kernel,
        grid_spec=pltpu.PrefetchScalarGridSpec(
            num_scalar_prefetch=2,
            grid=(b, s // tq),
            in_specs=[pl.BlockSpec((1, nh, tq, LANES), lambda bi, i, *_: (bi, 0, i, 0)),
                      kv_spec, kv_spec],
            out_specs=pl.BlockSpec((1, tq, nh * HEAD_DIM), lambda bi, i, *_: (bi, i, 0)),
            scratch_shapes=[pltpu.VMEM((nh + 1, tq, LANES), F32)] * 2,
        ),
        out_shape=jax.ShapeDtypeStruct((b, s, D_ATTN), F32),
        compiler_params=pltpu.CompilerParams(
            dimension_semantics=("arbitrary", "arbitrary"),
            vmem_limit_bytes=VMEM_LIMIT_BYTES),
        name="fox_attn",
    )(items, ntrips, q, k, v)


def _tail_kernel(x_ref, a_ref, nm_ref, p_ref, gattn_ref, wo_ref, gpost_ref, gffn_ref,
                 wg_ref, wu_ref, wd_ref, gffpost_ref, wpp_ref, gple_ref, wpg_ref, o_ref):
    na = _rms(a_ref[...], gattn_ref[...]).astype(BF16)
    mix = (jnp.dot(na, wo_ref[:D_ATTN, :], preferred_element_type=F32)
           + jnp.dot(nm_ref[...], wo_ref[D_ATTN:, :], preferred_element_type=F32))
    h = x_ref[...] + _rms(mix, gpost_ref[...])

    hn = _rms(h, gffn_ref[...]).astype(BF16)
    ff = jnp.zeros(h.shape, F32)
    for c0 in range(0, D_FF, FF_CHUNK):
        cols = slice(c0, c0 + FF_CHUNK)
        gate = jnp.dot(hn, wg_ref[:, cols], preferred_element_type=F32)
        up = jnp.dot(hn, wu_ref[:, cols], preferred_element_type=F32)
        act = (gate * jax.nn.sigmoid(gate) * up).astype(BF16)
        ff = ff + jnp.dot(act, wd_ref[cols, :], preferred_element_type=F32)
    h = h + _rms(ff, gffpost_ref[...])

    e = _rms(jnp.dot(p_ref[...].astype(BF16), wpp_ref[...], preferred_element_type=F32),
             gple_ref[...])
    gate = jax.nn.sigmoid(jnp.dot(h.astype(BF16), wpg_ref[...], preferred_element_type=F32))
    o_ref[...] = h + gate * e


def _tail_call(x2, a2, nm2, p2, gattn, wo, gpost, gffn, wg, wu, wd, gffpost, wpp, gple, wpg):
    t = x2.shape[0]
    tm = TAIL_ROWS
    rows = lambda width: pl.BlockSpec((tm, width), lambda i: (i, 0))
    const = lambda *shape: pl.BlockSpec(shape, lambda i: (0,) * len(shape),
                                        pipeline_mode=pl.Buffered(1))
    return pl.pallas_call(
        _tail_kernel,
        grid=(t // tm,),
        in_specs=[
            rows(D_MODEL), rows(D_ATTN), rows(D_POOL), rows(D_PLE),
            const(1, D_ATTN), const(D_MODEL, D_MODEL), const(1, D_MODEL), const(1, D_MODEL),
            const(D_MODEL, D_FF), const(D_MODEL, D_FF), const(D_FF, D_MODEL), const(1, D_MODEL),
            const(D_PLE, D_MODEL), const(1, D_MODEL), const(D_MODEL, D_MODEL),
        ],
        out_specs=rows(D_MODEL),
        out_shape=jax.ShapeDtypeStruct((t, D_MODEL), F32),
        compiler_params=pltpu.CompilerParams(
            dimension_semantics=("arbitrary",),
            vmem_limit_bytes=VMEM_LIMIT_BYTES),
        name="tail",
    )(x2, a2, nm2, p2, gattn, wo, gpost, gffn, wg, wu, wd, gffpost, wpp, gple, wpg)


def _selection_matrix():
    sel = np.zeros((LANES, ATTN_HEADS * LANES), np.float32)
    for h in range(ATTN_HEADS):
        for r in range(N_PIECES):
            sel[r * ATTN_HEADS + h, h * LANES + _aug_lane(h) + r] = 1.0
    return jnp.asarray(sel, BF16)


def _head_sum_matrix():
    hsum = np.zeros((2 * D_ATTN, LANES), np.float32)
    for d in range(2 * D_ATTN):
        hsum[d, d // HEAD_DIM] = 1.0
    return jnp.asarray(hsum, BF16)


def kernel(x, p, g_mix_pre, w_in, b_forget, g_attn_grp, g_pool_grp, w_pool, pool_scale, w_out,
           g_mix_post, g_ffn_pre, w_ffn_gate, w_ffn_up, w_ffn_down, g_ffn_post, w_ple_proj,
           g_ple, w_ple_gate):
    b, s, _ = x.shape
    depth = w_in.shape[0]
    assert s % PROJ_ROWS == 0 and PROJ_ROWS == ATTN_Q_ROWS and (b * s) % TAIL_ROWS == 0
    assert (s // ATTN_Q_ROWS * ATTN_HEADS) % ITEMS_PER_TRIP == 0
    sel = _selection_matrix()
    hsum = _head_sum_matrix()
    row = lambda v: v.reshape(1, -1).astype(F32)
    h = x
    for i in range(depth):
        w = w_in[i]
        o_f = 3 * D_ATTN
        o_u = o_f + ATTN_HEADS
        wf = w[:, o_f:o_u]
        w1 = jnp.concatenate(
            [w[:, :D_ATTN] * (1.0 / math.sqrt(HEAD_DIM)), w[:, D_ATTN:o_f], w[:, o_u:],
             wf, wf, wf, jnp.zeros((D_MODEL, LANES - N_PIECES * ATTN_HEADS), F32)],
            axis=1).astype(BF16)
        bf = b_forget[i].astype(F32)
        bf3 = jnp.concatenate([bf, bf, bf, jnp.zeros((LANES - N_PIECES * ATTN_HEADS,), F32)])
        q, k, v, nm, stats = _proj_call(
            h, row(g_mix_pre[i]), w1, bf3.reshape(1, LANES), sel, hsum,
            w_pool[i].astype(BF16), row(pool_scale[i]), row(g_pool_grp[i]))
        a = _attn_call(q, k, v, stats)
        t = b * s
        h = _tail_call(
            h.reshape(t, D_MODEL), a.reshape(t, D_ATTN), nm.reshape(t, D_POOL),
            p[i].reshape(t, D_PLE), row(g_attn_grp[i]), w_out[i].astype(BF16),
            row(g_mix_post[i]), row(g_ffn_pre[i]), w_ffn_gate[i].astype(BF16),
            w_ffn_up[i].astype(BF16), w_ffn_down[i].astype(BF16), row(g_ffn_post[i]),
            w_ple_proj[i].astype(BF16), row(g_ple[i]), w_ple_gate[i].astype(BF16),
        ).reshape(b, s, D_MODEL)
    return h
```

```python
import functools
import math

import jax
import jax.numpy as jnp
import numpy as np
from jax import lax
from jax.experimental import pallas as pl
from jax.experimental.pallas import tpu as pltpu

D_MODEL = 1024
D_PLE = 256
ATTN_HEADS = 8
HEAD_DIM = 64
D_ATTN = ATTN_HEADS * HEAD_DIM
POOL_WINDOWS = (2, 4, 8, 16)
POOL_GROUPS = len(POOL_WINDOWS)
POOL_CH = 128
D_POOL = POOL_GROUPS * POOL_CH
D_FF = 2816
RMS_EPS = 1e-6
LOG2E = math.log2(math.e)

LANES = 128
HALO = 16
CUMSUM_CHUNK = 256
N_PIECES = 3
W1_COLS = 3 * D_ATTN + D_POOL + LANES

PROJ_ROWS = 512
ATTN_Q_ROWS = 512
ITEMS_PER_TRIP = 8
ITEMS_PER_SHORT_TRIP = 4
STATS_ROWS = 8
SKIP_LOG2 = 160.0
NORM_SLACK = 1.01
TAIL_ROWS = 512
FF_CHUNK = 256
VMEM_LIMIT_BYTES = 56 * 1024 * 1024

BF16 = jnp.bfloat16
F32 = jnp.float32


def _aug_lane(head):
    return HEAD_DIM if head % 2 == 0 else 0


def _rms(x, g):
    ms = jnp.mean(x * x, axis=-1, keepdims=True)
    return x * lax.rsqrt(ms + RMS_EPS) * g


def _split3(x):
    hi = x.astype(BF16)
    r1 = x - hi.astype(F32)
    mid = r1.astype(BF16)
    lo = (r1 - mid.astype(F32)).astype(BF16)
    return hi, mid, lo


def _proj_kernel(x_ref, g_ref, w1_ref, bf_ref, sel_ref, hsum_ref, wpool_ref, pscale_ref, gpool_ref,
                 q_ref, k_ref, v_ref, nm_ref, stats_ref, carry_ref, halo_ref):
    i = pl.program_id(1)
    tm = x_ref.shape[1]

    @pl.when(i == 0)
    def _():
        carry_ref[...] = jnp.zeros_like(carry_ref)
        halo_ref[...] = jnp.zeros_like(halo_ref)

    hn = _rms(x_ref[0], g_ref[...]).astype(BF16)
    z = jnp.dot(hn, w1_ref[...], preferred_element_type=F32)

    lane = lax.broadcasted_iota(jnp.int32, (tm, LANES), 1)

    zf = z[:, 3 * D_ATTN + D_POOL:] + bf_ref[...]
    logf = jnp.minimum(zf, 0.0) - jnp.log1p(jnp.exp(-jnp.abs(zf)))
    logf = jnp.where(lane < N_PIECES * ATTN_HEADS, logf, 0.0)
    pieces = _split3(logf)
    r = lax.broadcasted_iota(jnp.int32, (CUMSUM_CHUNK, CUMSUM_CHUNK), 0)
    c = lax.broadcasted_iota(jnp.int32, (CUMSUM_CHUNK, CUMSUM_CHUNK), 1)
    tri = jnp.where(c <= r, 1.0, 0.0).astype(BF16)
    carry = carry_ref[0:1, :]
    chunks = []
    for r0 in range(0, tm, CUMSUM_CHUNK):
        cs = carry
        for piece in pieces:
            cs = cs + jnp.dot(tri, piece[r0:r0 + CUMSUM_CHUNK], preferred_element_type=F32)
        carry = cs[CUMSUM_CHUNK - 1:CUMSUM_CHUNK, :]
        chunks.append(cs)
    carry_ref[0:1, :] = carry
    negc = -LOG2E * jnp.concatenate(chunks, axis=0)
    nhi, nmid, nlo = _split3(negc)
    cp = jnp.where(lane < ATTN_HEADS, nhi.astype(F32),
                   jnp.where(lane < 2 * ATTN_HEADS, nmid.astype(F32), nlo.astype(F32))).astype(BF16)
    kaug_all = jnp.dot(cp, sel_ref[...], preferred_element_type=F32)

    for h in range(ATTN_HEADS):
        slab = slice((h // 2) * LANES, (h // 2 + 1) * LANES)
        a0 = _aug_lane(h)
        data = (lane < HEAD_DIM) if h % 2 == 0 else (lane >= HEAD_DIM)
        ones3 = jnp.where((lane >= a0) & (lane < a0 + N_PIECES), 1.0, 0.0)
        one1 = jnp.where(lane == a0, 1.0, 0.0)
        kaug = kaug_all[:, h * LANES:(h + 1) * LANES]
        q_ref[0, h] = jnp.where(data, LOG2E * z[:, 0 * D_ATTN:1 * D_ATTN][:, slab],
                                ones3).astype(BF16)
        k_ref[0, h] = jnp.where(data, z[:, 1 * D_ATTN:2 * D_ATTN][:, slab], kaug).astype(BF16)
        v_ref[0, h] = jnp.where(data, z[:, 2 * D_ATTN:3 * D_ATTN][:, slab], one1).astype(BF16)

    zqk = z[:, :2 * D_ATTN]
    sqn = jnp.dot((zqk * zqk).astype(BF16), hsum_ref[...], preferred_element_type=F32)
    stats_ref[0, 0, 0:1, :] = negc[0:1, :]
    stats_ref[0, 0, 1:2, :] = negc[tm - 1:tm, :]
    stats_ref[0, 0, 2:3, :] = jnp.max(sqn, axis=0, keepdims=True)
    stats_ref[0, 0, 3:8, :] = jnp.zeros((5, LANES), F32)

    u = z[:, 3 * D_ATTN:3 * D_ATTN + D_POOL]
    ue = jnp.concatenate([halo_ref[...], u], axis=0)
    halo_ref[...] = u[tm - HALO:, :]
    pos = i * tm + lax.broadcasted_iota(jnp.int32, (tm, POOL_CH), 0)
    ms = []
    for g, w in enumerate(POOL_WINDOWS):
        ch = slice(g * POOL_CH, (g + 1) * POOL_CH)
        s = ue[:, ch]
        shift = 1
        while shift < w:
            s = s + pltpu.roll(s, shift, axis=0)
            shift *= 2
        inv_count = 1.0 / jnp.minimum(pos + 1, w).astype(F32)
        y = s[HALO:] * inv_count - u[:, ch]
        m = jnp.dot(y.astype(BF16), wpool_ref[g], preferred_element_type=F32)
        ms.append(m * pscale_ref[:, ch])
    msq = sum(jnp.sum(m * m, axis=-1, keepdims=True) for m in ms) * (1.0 / D_POOL)
    rstd = lax.rsqrt(msq + RMS_EPS)
    for g in range(POOL_GROUPS):
        ch = slice(g * POOL_CH, (g + 1) * POOL_CH)
        nm_ref[0, :, ch] = (ms[g] * rstd * gpool_ref[:, ch]).astype(BF16)


def _proj_call(x, g_pre, w1, bf3, sel, hsum, wpool, pscale, gpool):
    b, s, _ = x.shape
    tm = PROJ_ROWS
    const = lambda *shape: pl.BlockSpec(shape, lambda bi, i: (0,) * len(shape))
    head_spec = pl.BlockSpec((1, ATTN_HEADS, tm, LANES), lambda bi, i: (bi, 0, i, 0))
    head_shape = jax.ShapeDtypeStruct((b, ATTN_HEADS, s, LANES), BF16)
    return pl.pallas_call(
        _proj_kernel,
        grid=(b, s // tm),
        in_specs=[
            pl.BlockSpec((1, tm, D_MODEL), lambda bi, i: (bi, i, 0)),
            const(1, D_MODEL),
            const(D_MODEL, W1_COLS),
            const(1, LANES),
            const(LANES, ATTN_HEADS * LANES),
            const(2 * D_ATTN, LANES),
            const(POOL_GROUPS, POOL_CH, POOL_CH),
            const(1, D_POOL),
            const(1, D_POOL),
        ],
        out_specs=[head_spec, head_spec, head_spec,
                   pl.BlockSpec((1, tm, D_POOL), lambda bi, i: (bi, i, 0)),
                   pl.BlockSpec((1, 1, STATS_ROWS, LANES), lambda bi, i: (bi, i, 0, 0))],
        out_shape=[head_shape, head_shape, head_shape,
                   jax.ShapeDtypeStruct((b, s, D_POOL), BF16),
                   jax.ShapeDtypeStruct((b, s // tm, STATS_ROWS, LANES), F32)],
        scratch_shapes=[pltpu.VMEM((8, LANES), F32), pltpu.VMEM((HALO, D_POOL), F32)],
        compiler_params=pltpu.CompilerParams(
            dimension_semantics=("arbitrary", "arbitrary"),
            vmem_limit_bytes=VMEM_LIMIT_BYTES),
        name="proj",
    )(x, g_pre, w1, bf3, sel, hsum, wpool, pscale, gpool)


def _attn_kernel(items_ref, ntrips_ref, q_ref, k_ref, v_ref, o_ref, m_ref, acc_ref):
    step_id = pl.program_id(0) * pl.num_programs(1) + pl.program_id(1)
    i = pl.program_id(1)
    nh = q_ref.shape[1]
    tq = q_ref.shape[2]
    n_codes = pl.num_programs(1) * nh
    nt = (((1,), (1,)), ((), ()))
    lane = lax.broadcasted_iota(jnp.int32, (tq, LANES), 1)

    m_ref[...] = jnp.full(m_ref.shape, -jnp.inf, F32)
    acc_ref[...] = jnp.zeros(acc_ref.shape, F32)

    def qk(item):
        head, _, q_rows, k_rows, _ = item
        return lax.dot_general(q_ref[0, head, q_rows, :], k_ref[0, head, k_rows, :], nt,
                               preferred_element_type=F32)

    def run(items, ahead):
        scores = [qk(item) for item in items[:ahead]]
        for n, (head, slot, q_rows, k_rows, mask) in enumerate(items):
            s = scores.pop(0)
            if n + ahead < len(items):
                scores.append(qk(items[n + ahead]))
            if mask is not None:
                s = jnp.where(mask, s, -jnp.inf)
            m = m_ref[slot, q_rows, :]
            m_new = jnp.maximum(m, jnp.max(s, axis=-1, keepdims=True))
            p = jnp.exp2(s - jnp.concatenate([m_new] * (s.shape[1] // LANES), axis=1))
            alpha = jnp.exp2(m - m_new)
            acc_ref[slot, q_rows, :] = alpha * acc_ref[slot, q_rows, :] + jnp.dot(
                p.astype(BF16), v_ref[0, head, k_rows, :], preferred_element_type=F32)
            m_ref[slot, q_rows, :] = m_new

    all_rows = slice(0, tq)
    first_item = step_id * n_codes

    def scheduled(start, count):
        items = []
        for n in range(count):
            code = items_ref[start + n]
            valid = code < n_codes
            head = jnp.where(valid, code & (nh - 1), 0)
            block = jnp.where(valid, lax.shift_right_logical(code, nh.bit_length() - 1), 0)
            items.append((head, jnp.where(valid, head, nh), all_rows,
                          pl.ds(pl.multiple_of(block * tq, tq), tq), None))
        run(items, ahead=2)

    n_long = ntrips_ref[2 * step_id]

    @pl.loop(0, n_long)
    def _(trip):
        scheduled(first_item + trip * ITEMS_PER_TRIP, ITEMS_PER_TRIP)

    @pl.loop(0, ntrips_ref[2 * step_id + 1])
    def _(trip):
        scheduled(first_item + n_long * ITEMS_PER_TRIP, ITEMS_PER_SHORT_TRIP)

    r = lax.broadcasted_iota(jnp.int32, (tq, tq), 0)
    c = lax.broadcasted_iota(jnp.int32, (tq, tq), 1)
    own_keys = pl.ds(pl.multiple_of(i * tq, tq), tq)
    run([(hh, hh, all_rows, own_keys, c <= r) for hh in range(nh)], ahead=2)
    for pair in range(nh // 2):
        outs = []
        for hh in (2 * pair, 2 * pair + 1):
            acc = acc_ref[hh]
            l_lane = _aug_lane(hh)
            outs.append(acc / acc[:, l_lane:l_lane + 1])
        o_ref[0, :, pair * LANES:(pair + 1) * LANES] = jnp.where(lane < HEAD_DIM, outs[0], outs[1])


def _attention_schedule(stats):
    nb = stats.shape[1]
    nh = ATTN_HEADS
    neg_first = stats[:, :, 0, :nh]
    neg_last = stats[:, :, 1, :nh]
    qn = jnp.sqrt(stats[:, :, 2, :nh]) * (LOG2E * NORM_SLACK)
    kn = jnp.sqrt(stats[:, :, 2, nh:2 * nh]) * NORM_SLACK
    top = neg_last[:, None, :, :] + qn[:, :, None, :] * kn[:, None, :, :]
    floor = neg_first[:, :, None, :] - (qn * kn)[:, :, None, :]
    tile_i = lax.broadcasted_iota(jnp.int32, (1, nb, nb, nh), 1)
    block_j = lax.broadcasted_iota(jnp.int32, (1, nb, nb, nh), 2)
    head = lax.broadcasted_iota(jnp.int32, (1, nb, nb, nh), 3)
    need = (block_j < tile_i) & ~(top - floor <= -SKIP_LOG2)
    n_codes = nb * nh
    codes = jnp.where(need, block_j * nh + head, n_codes).reshape(stats.shape[0], nb, n_codes)
    items = jnp.sort(codes, axis=-1).astype(jnp.int32)
    count = jnp.sum(need, axis=(2, 3)).astype(jnp.int32)
    rest = count % ITEMS_PER_TRIP
    n_long = count // ITEMS_PER_TRIP + (rest > ITEMS_PER_SHORT_TRIP)
    n_short = (rest > 0) & (rest <= ITEMS_PER_SHORT_TRIP)
    ntrips = jnp.stack([n_long, n_short.astype(jnp.int32)], axis=-1)
    return items.reshape(-1), ntrips.reshape(-1)


def _attn_call(q, k, v, stats):
    b, nh, s, _ = q.shape
    tq = ATTN_Q_ROWS
    items, ntrips = _attention_schedule(stats)
    kv_spec = pl.BlockSpec((1, nh, s, LANES), lambda bi, i, *_: (bi, 0, 0, 0))
    return pl.pallas_call(
        _attn_kernel,
        grid_spec=pltpu.PrefetchScalarGridSpec(
            num_scalar_prefetch=2,
            grid=(b, s // tq),
            in_specs=[pl.BlockSpec((1, nh, tq, LANES), lambda bi, i, *_: (bi, 0, i, 0)),
                      kv_spec, kv_spec],
            out_specs=pl.BlockSpec((1, tq, nh * HEAD_DIM), lambda bi, i, *_: (bi, i, 0)),
            scratch_shapes=[pltpu.VMEM((nh + 1, tq, LANES), F32)] * 2,
        ),
        out_shape=jax.ShapeDtypeStruct((b, s, D_ATTN), F32),
        compiler_params=pltpu.CompilerParams(
            dimension_semantics=("arbitrary", "arbitrary"),
            vmem_limit_bytes=VMEM_LIMIT_BYTES),
        name="fox_attn",
    )(items, ntrips, q, k, v)


def _tail_kernel(x_ref, a_ref, nm_ref, p_ref, gattn_ref, wo_ref, gpost_ref, gffn_ref,
                 wg_ref, wu_ref, wd_ref, gffpost_ref, wpp_ref, gple_ref, wpg_ref, o_ref):
    na = _rms(a_ref[...], gattn_ref[...]).astype(BF16)
    mix = (jnp.dot(na, wo_ref[:D_ATTN, :], preferred_element_type=F32)
           + jnp.dot(nm_ref[...], wo_ref[D_ATTN:, :], preferred_element_type=F32))
    h = x_ref[...] + _rms(mix, gpost_ref[...])

    hn = _rms(h, gffn_ref[...]).astype(BF16)
    ff = jnp.zeros(h.shape, F32)
    for c0 in range(0, D_FF, FF_CHUNK):
        cols = slice(c0, c0 + FF_CHUNK)
        gate = jnp.dot(hn, wg_ref[:, cols], preferred_element_type=F32)
        up = jnp.dot(hn, wu_ref[:, cols], preferred_element_type=F32)
        act = (gate * jax.nn.sigmoid(gate) * up).astype(BF16)
        ff = ff + jnp.dot(act, wd_ref[cols, :], preferred_element_type=F32)
    h = h + _rms(ff, gffpost_ref[...])

    e = _rms(jnp.dot(p_ref[...].astype(BF16), wpp_ref[...], preferred_element_type=F32),
             gple_ref[...])
    gate = jax.nn.sigmoid(jnp.dot(h.astype(BF16), wpg_ref[...], preferred_element_type=F32))
    o_ref[...] = h + gate * e


def _tail_call(x2, a2, nm2, p2, gattn, wo, gpost, gffn, wg, wu, wd, gffpost, wpp, gple, wpg):
    t = x2.shape[0]
    tm = TAIL_ROWS
    rows = lambda width: pl.BlockSpec((tm, width), lambda i: (i, 0))
    const = lambda *shape: pl.BlockSpec(shape, lambda i: (0,) * len(shape),
                                        pipeline_mode=pl.Buffered(1))
    return pl.pallas_call(
        _tail_kernel,
        grid=(t // tm,),
        in_specs=[
            rows(D_MODEL), rows(D_ATTN), rows(D_POOL), rows(D_PLE),
            const(1, D_ATTN), const(D_MODEL, D_MODEL), const(1, D_MODEL), const(1, D_MODEL),
            const(D_MODEL, D_FF), const(D_MODEL, D_FF), const(D_FF, D_MODEL), const(1, D_MODEL),
            const(D_PLE, D_MODEL), const(1, D_MODEL), const(D_MODEL, D_MODEL),
        ],
        out_specs=rows(D_MODEL),
        out_shape=jax.ShapeDtypeStruct((t, D_MODEL), F32),
        compiler_params=pltpu.CompilerParams(
            dimension_semantics=("arbitrary",),
            vmem_limit_bytes=VMEM_LIMIT_BYTES,
            allow_input_fusion=[False] * 4 + [False, True, False, False, True, True, True,
                                              False, True, False, True]),
        name="tail",
    )(x2, a2, nm2, p2, gattn, wo, gpost, gffn, wg, wu, wd, gffpost, wpp, gple, wpg)


def _selection_matrix():
    sel = np.zeros((LANES, ATTN_HEADS * LANES), np.float32)
    for h in range(ATTN_HEADS):
        for r in range(N_PIECES):
            sel[r * ATTN_HEADS + h, h * LANES + _aug_lane(h) + r] = 1.0
    return jnp.asarray(sel, BF16)


def _head_sum_matrix():
    hsum = np.zeros((2 * D_ATTN, LANES), np.float32)
    for d in range(2 * D_ATTN):
        hsum[d, d // HEAD_DIM] = 1.0
    return jnp.asarray(hsum, BF16)


def kernel(x, p, g_mix_pre, w_in, b_forget, g_attn_grp, g_pool_grp, w_pool, pool_scale, w_out,
           g_mix_post, g_ffn_pre, w_ffn_gate, w_ffn_up, w_ffn_down, g_ffn_post, w_ple_proj,
           g_ple, w_ple_gate):
    b, s, _ = x.shape
    depth = w_in.shape[0]
    assert s % PROJ_ROWS == 0 and PROJ_ROWS == ATTN_Q_ROWS and (b * s) % TAIL_ROWS == 0
    assert (s // ATTN_Q_ROWS * ATTN_HEADS) % ITEMS_PER_TRIP == 0
    sel = _selection_matrix()
    hsum = _head_sum_matrix()
    row = lambda v: v.reshape(1, -1).astype(F32)
    h = x
    for i in range(depth):
        w = w_in[i]
        o_f = 3 * D_ATTN
        o_u = o_f + ATTN_HEADS
        wf = w[:, o_f:o_u]
        w1 = jnp.concatenate(
            [w[:, :D_ATTN] * (1.0 / math.sqrt(HEAD_DIM)), w[:, D_ATTN:o_f], w[:, o_u:],
             wf, wf, wf, jnp.zeros((D_MODEL, LANES - N_PIECES * ATTN_HEADS), F32)],
            axis=1).astype(BF16)
        bf = b_forget[i].astype(F32)
        bf3 = jnp.concatenate([bf, bf, bf, jnp.zeros((LANES - N_PIECES * ATTN_HEADS,), F32)])
        q, k, v, nm, stats = _proj_call(
            h, row(g_mix_pre[i]), w1, bf3.reshape(1, LANES), sel, hsum,
            w_pool[i].astype(BF16), row(pool_scale[i]), row(g_pool_grp[i]))
        a = _attn_call(q, k, v, stats)
        t = b * s
        h = _tail_call(
            h.reshape(t, D_MODEL), a.reshape(t, D_ATTN), nm.reshape(t, D_POOL),
            p[i].reshape(t, D_PLE), row(g_attn_grp[i]), w_out[i].astype(BF16),
            row(g_mix_post[i]), row(g_ffn_pre[i]), w_ffn_gate[i].astype(BF16),
            w_ffn_up[i].astype(BF16), w_ffn_down[i].astype(BF16), row(g_ffn_post[i]),
            w_ple_proj[i].astype(BF16), row(g_ple[i]), w_ple_gate[i].astype(BF16),
        ).reshape(b, s, D_MODEL)
    return h
```

```python
import math

import jax
import jax.numpy as jnp
import numpy as np
from jax import lax
from jax.experimental import pallas as pl
from jax.experimental.pallas import tpu as pltpu

D_MODEL = 1024
D_PLE = 256
ATTN_HEADS = 8
HEAD_DIM = 64
D_ATTN = ATTN_HEADS * HEAD_DIM
POOL_WINDOWS = (2, 4, 8, 16)
POOL_GROUPS = len(POOL_WINDOWS)
POOL_CH = 128
D_POOL = POOL_GROUPS * POOL_CH
D_FF = 2816
RMS_EPS = 1e-6
LOG2E = math.log2(math.e)

LANES = 128
HALO = 16
CUMSUM_CHUNK = 256
N_PIECES = 3
PIECE_LANES = N_PIECES * ATTN_HEADS
U_COL = LANES
QKV_COL = U_COL + D_POOL
W1_COLS = QKV_COL + 3 * D_ATTN

PROJ_ROWS = 512
ATTN_Q_ROWS = 512
ITEMS_PER_TRIP = 8
ITEMS_PER_SHORT_TRIP = 4
STATS_ROWS = 8
SKIP_LOG2 = 160.0
NORM_SLACK = 1.01
TAIL_ROWS = 512
FF_CHUNK = 256
VMEM_LIMIT_BYTES = 56 * 1024 * 1024

BF16 = jnp.bfloat16
F32 = jnp.float32


def _aug_lane(head):
    return HEAD_DIM if head % 2 == 0 else 0


def _rms(x, g):
    ms = jnp.mean(x * x, axis=-1, keepdims=True)
    return x * lax.rsqrt(ms + RMS_EPS) * g


def _split3(x):
    hi = x.astype(BF16).astype(F32)
    r1 = x - hi
    mid = r1.astype(BF16).astype(F32)
    lo = (r1 - mid).astype(BF16).astype(F32)
    return hi, mid, lo


def _by_piece(lane, first, second, third):
    return jnp.where(lane < ATTN_HEADS, first, jnp.where(lane < 2 * ATTN_HEADS, second, third))


def _proj_kernel(x_ref, g_ref, w1_ref, bf_ref, hsum_ref, wpool_ref, pscale_ref, gpool_ref,
                 q_ref, k_ref, v_ref, nm_ref, stats_ref, carry_ref, halo_ref):
    tile = pl.program_id(1)
    tm = x_ref.shape[1]

    @pl.when(tile == 0)
    def _():
        carry_ref[...] = jnp.zeros_like(carry_ref)
        halo_ref[...] = jnp.zeros_like(halo_ref)

    hn = _rms(x_ref[0], g_ref[...]).astype(BF16)
    z = jnp.dot(hn, w1_ref[...], preferred_element_type=F32)
    lane = lax.broadcasted_iota(jnp.int32, (tm, LANES), 1)

    zf = z[:, :U_COL] + bf_ref[...]
    logf = jnp.minimum(zf, 0.0) - jnp.log1p(jnp.exp(-jnp.abs(zf)))
    logf = jnp.where(lane < PIECE_LANES, logf, 0.0)
    pieces = _by_piece(lane, *_split3(logf)).astype(BF16)
    r = lax.broadcasted_iota(jnp.int32, (CUMSUM_CHUNK, CUMSUM_CHUNK), 0)
    c = lax.broadcasted_iota(jnp.int32, (CUMSUM_CHUNK, CUMSUM_CHUNK), 1)
    tri = jnp.where(c <= r, 1.0, 0.0).astype(BF16)
    carry = carry_ref[0:1, :]
    chunks = []
    for r0 in range(0, tm, CUMSUM_CHUNK):
        cs = carry + jnp.dot(tri, pieces[r0:r0 + CUMSUM_CHUNK], preferred_element_type=F32)
        carry = cs[CUMSUM_CHUNK - 1:CUMSUM_CHUNK, :]
        chunks.append(cs)
    carry_ref[0:1, :] = carry
    part = jnp.concatenate(chunks, axis=0)
    total = (part + pltpu.roll(part, LANES - ATTN_HEADS, axis=1)
             + pltpu.roll(part, LANES - 2 * ATTN_HEADS, axis=1))
    negc = -LOG2E * total
    negc3 = _by_piece(lane, negc, pltpu.roll(negc, ATTN_HEADS, axis=1),
                      pltpu.roll(negc, 2 * ATTN_HEADS, axis=1))
    cp_odd = jnp.where(lane < PIECE_LANES, _by_piece(lane, *_split3(negc3)), 0.0)
    cp_even = pltpu.roll(cp_odd, HEAD_DIM, axis=1)

    for h in range(ATTN_HEADS):
        first = QKV_COL + (h // 2) * LANES
        slab = lambda section: z[:, section * D_ATTN + first:section * D_ATTN + first + LANES]
        a0 = _aug_lane(h)
        data = (lane < HEAD_DIM) if h % 2 == 0 else (lane >= HEAD_DIM)
        own = ((lane == a0 + h) | (lane == a0 + ATTN_HEADS + h)
               | (lane == a0 + 2 * ATTN_HEADS + h))
        q_ref[0, h] = jnp.where(data, LOG2E * slab(0), jnp.where(own, 1.0, 0.0)).astype(BF16)
        k_ref[0, h] = jnp.where(data, slab(1), cp_even if h % 2 == 0 else cp_odd).astype(BF16)
        v_ref[0, h] = jnp.where(data, slab(2), jnp.where(lane == a0, 1.0, 0.0)).astype(BF16)

    zqk = z[:, QKV_COL:QKV_COL + 2 * D_ATTN]
    sqn = jnp.dot((zqk * zqk).astype(BF16), hsum_ref[...], preferred_element_type=F32)
    stats_ref[0, 0, 0:1, :] = negc[0:1, :]
    stats_ref[0, 0, 1:2, :] = negc[tm - 1:tm, :]
    stats_ref[0, 0, 2:3, :] = jnp.max(sqn, axis=0, keepdims=True)
    stats_ref[0, 0, 3:8, :] = jnp.zeros((5, LANES), F32)

    u = z[:, U_COL:QKV_COL]
    ue = jnp.concatenate([halo_ref[...], u], axis=0)
    halo_ref[...] = u[tm - HALO:, :]
    pos = tile * tm + lax.broadcasted_iota(jnp.int32, (tm, POOL_CH), 0)
    ys = []
    for g, w in enumerate(POOL_WINDOWS):
        ch = slice(g * POOL_CH, (g + 1) * POOL_CH)
        s = ue[:, ch]
        shift = 1
        while shift < w:
            s = s + pltpu.roll(s, shift, axis=0)
            shift *= 2
        inv_count = 1.0 / jnp.minimum(pos + 1, w).astype(F32)
        ys.append((s[HALO:] * inv_count - u[:, ch]).astype(BF16))
    ms = []
    for pair in range(POOL_GROUPS // 2):
        cols = slice(2 * pair * POOL_CH, (2 * pair + 2) * POOL_CH)
        m2 = jnp.dot(jnp.concatenate(ys[2 * pair:2 * pair + 2], axis=1), wpool_ref[pair],
                     preferred_element_type=F32) * pscale_ref[:, cols]
        ms += [m2[:, :POOL_CH], m2[:, POOL_CH:]]
    msq = sum(jnp.sum(m * m, axis=-1, keepdims=True) for m in ms) * (1.0 / D_POOL)
    rstd = lax.rsqrt(msq + RMS_EPS)
    for g in range(POOL_GROUPS):
        ch = slice(g * POOL_CH, (g + 1) * POOL_CH)
        nm_ref[0, :, ch] = (ms[g] * rstd * gpool_ref[:, ch]).astype(BF16)


def _proj_call(x, g_pre, w1, bf3, hsum, wpool2, pscale, gpool):
    b, s, _ = x.shape
    tm = PROJ_ROWS
    const = lambda *shape: pl.BlockSpec(shape, lambda bi, i: (0,) * len(shape))
    head_spec = pl.BlockSpec((1, ATTN_HEADS, tm, LANES), lambda bi, i: (bi, 0, i, 0))
    head_shape = jax.ShapeDtypeStruct((b, ATTN_HEADS, s, LANES), BF16)
    return pl.pallas_call(
        _proj_kernel,
        grid=(b, s // tm),
        in_specs=[
            pl.BlockSpec((1, tm, D_MODEL), lambda bi, i: (bi, i, 0)),
            const(1, D_MODEL),
            const(D_MODEL, W1_COLS),
            const(1, LANES),
            const(2 * D_ATTN, LANES),
            const(POOL_GROUPS // 2, 2 * POOL_CH, 2 * POOL_CH),
            const(1, D_POOL),
            const(1, D_POOL),
        ],
        out_specs=[head_spec, head_spec, head_spec,
                   pl.BlockSpec((1, tm, D_POOL), lambda bi, i: (bi, i, 0)),
                   pl.BlockSpec((1, 1, STATS_ROWS, LANES), lambda bi, i: (bi, i, 0, 0))],
        out_shape=[head_shape, head_shape, head_shape,
                   jax.ShapeDtypeStruct((b, s, D_POOL), BF16),
                   jax.ShapeDtypeStruct((b, s // tm, STATS_ROWS, LANES), F32)],
        scratch_shapes=[pltpu.VMEM((8, LANES), F32), pltpu.VMEM((HALO, D_POOL), F32)],
        compiler_params=pltpu.CompilerParams(
            dimension_semantics=("arbitrary", "arbitrary"),
            vmem_limit_bytes=VMEM_LIMIT_BYTES),
        name="proj",
    )(x, g_pre, w1, bf3, hsum, wpool2, pscale, gpool)


def _attn_kernel(items_ref, ntrips_ref, q_ref, k_ref, v_ref, o_ref, m_ref, acc_ref):
    step_id = pl.program_id(0) * pl.num_programs(1) + pl.program_id(1)
    i = pl.program_id(1)
    nh = q_ref.shape[1]
    tq = q_ref.shape[2]
    n_codes = pl.num_programs(1) * nh
    nt = (((1,), (1,)), ((), ()))
    lane = lax.broadcasted_iota(jnp.int32, (tq, LANES), 1)

    m_ref[...] = jnp.full(m_ref.shape, -jnp.inf, F32)
    acc_ref[...] = jnp.zeros(acc_ref.shape, F32)

    def qk(item):
        head, _, q_rows, k_rows, _ = item
        return lax.dot_general(q_ref[0, head, q_rows, :], k_ref[0, head, k_rows, :], nt,
                               preferred_element_type=F32)

    def run(items, ahead):
        scores = [qk(item) for item in items[:ahead]]
        for n, (head, slot, q_rows, k_rows, mask) in enumerate(items):
            s = scores.pop(0)
            if n + ahead < len(items):
                scores.append(qk(items[n + ahead]))
            if mask is not None:
                s = jnp.where(mask, s, -jnp.inf)
            m = m_ref[slot, q_rows, :]
            m_new = jnp.maximum(m, jnp.max(s, axis=-1, keepdims=True))
            p = jnp.exp2(s - jnp.concatenate([m_new] * (s.shape[1] // LANES), axis=1))
            alpha = jnp.exp2(m - m_new)
            acc_ref[slot, q_rows, :] = alpha * acc_ref[slot, q_rows, :] + jnp.dot(
                p.astype(BF16), v_ref[0, head, k_rows, :], preferred_element_type=F32)
            m_ref[slot, q_rows, :] = m_new

    all_rows = slice(0, tq)
    first_item = step_id * n_codes

    def scheduled(start, count):
        items = []
        for n in range(count):
            code = items_ref[start + n]
            valid = code < n_codes
            head = jnp.where(valid, code & (nh - 1), 0)
            block = jnp.where(valid, lax.shift_right_logical(code, nh.bit_length() - 1), 0)
            items.append((head, jnp.where(valid, head, nh), all_rows,
                          pl.ds(pl.multiple_of(block * tq, tq), tq), None))
        run(items, ahead=2)

    n_long = ntrips_ref[2 * step_id]

    @pl.loop(0, n_long)
    def _(trip):
        scheduled(first_item + trip * ITEMS_PER_TRIP, ITEMS_PER_TRIP)

    @pl.loop(0, ntrips_ref[2 * step_id + 1])
    def _(trip):
        scheduled(first_item + n_long * ITEMS_PER_TRIP, ITEMS_PER_SHORT_TRIP)

    r = lax.broadcasted_iota(jnp.int32, (tq, tq), 0)
    c = lax.broadcasted_iota(jnp.int32, (tq, tq), 1)
    own_keys = pl.ds(pl.multiple_of(i * tq, tq), tq)
    run([(hh, hh, all_rows, own_keys, c <= r) for hh in range(nh)], ahead=2)
    for pair in range(nh // 2):
        outs = []
        for hh in (2 * pair, 2 * pair + 1):
            acc = acc_ref[hh]
            l_lane = _aug_lane(hh)
            outs.append(acc / acc[:, l_lane:l_lane + 1])
        o_ref[0, :, pair * LANES:(pair + 1) * LANES] = jnp.where(lane < HEAD_DIM, outs[0], outs[1])


def _attention_schedule(stats):
    nb = stats.shape[1]
    nh = ATTN_HEADS
    neg_first = stats[:, :, 0, :nh]
    neg_last = stats[:, :, 1, :nh]
    qn = jnp.sqrt(stats[:, :, 2, :nh]) * (LOG2E * NORM_SLACK)
    kn = jnp.sqrt(stats[:, :, 2, nh:2 * nh]) * NORM_SLACK
    top = neg_last[:, None, :, :] + qn[:, :, None, :] * kn[:, None, :, :]
    floor = neg_first[:, :, None, :] - (qn * kn)[:, :, None, :]
    tile_i = lax.broadcasted_iota(jnp.int32, (1, nb, nb, nh), 1)
    block_j = lax.broadcasted_iota(jnp.int32, (1, nb, nb, nh), 2)
    head = lax.broadcasted_iota(jnp.int32, (1, nb, nb, nh), 3)
    need = (block_j < tile_i) & ~(top - floor <= -SKIP_LOG2)
    n_codes = nb * nh
    codes = jnp.where(need, block_j * nh + head, n_codes).reshape(stats.shape[0], nb, n_codes)
    items = jnp.sort(codes, axis=-1).astype(jnp.int32)
    count = jnp.sum(need, axis=(2, 3)).astype(jnp.int32)
    rest = count % ITEMS_PER_TRIP
    n_long = count // ITEMS_PER_TRIP + (rest > ITEMS_PER_SHORT_TRIP)
    n_short = (rest > 0) & (rest <= ITEMS_PER_SHORT_TRIP)
    ntrips = jnp.stack([n_long, n_short.astype(jnp.int32)], axis=-1)
    return items.reshape(-1), ntrips.reshape(-1)


def _attn_call(q, k, v, stats):
    b, nh, s, _ = q.shape
    tq = ATTN_Q_ROWS
    items, ntrips = _attention_schedule(stats)
    kv_spec = pl.BlockSpec((1, nh, s, LANES), lambda bi, i, *_: (bi, 0, 0, 0))
    return pl.pallas_call(
        _attn_kernel,
        grid_spec=pltpu.PrefetchScalarGridSpec(
            num_scalar_prefetch=2,
            grid=(b, s // tq),
            in_specs=[pl.BlockSpec((1, nh, tq, LANES), lambda bi, i, *_: (bi, 0, i, 0)),
                      kv_spec, kv_spec],
            out_specs=pl.BlockSpec((1, tq, nh * HEAD_DIM), lambda bi, i, *_: (bi, i, 0)),
            scratch_shapes=[pltpu.VMEM((nh + 1, tq, LANES), F32)] * 2,
        ),
        out_shape=jax.ShapeDtypeStruct((b, s, D_ATTN), F32),
        compiler_params=pltpu.CompilerParams(
            dimension_semantics=("arbitrary", "arbitrary"),
            vmem_limit_bytes=VMEM_LIMIT_BYTES),
        name="fox_attn",
    )(items, ntrips, q, k, v)


def _tail_kernel(x_ref, a_ref, nm_ref, p_ref, gattn_ref, wo_ref, gpost_ref, gffn_ref,
                 wg_ref, wu_ref, wd_ref, gffpost_ref, wpp_ref, gple_ref, wpg_ref, o_ref):
    na = _rms(a_ref[...], gattn_ref[...]).astype(BF16)
    mix = (jnp.dot(na, wo_ref[:D_ATTN, :], preferred_element_type=F32)
           + jnp.dot(nm_ref[...], wo_ref[D_ATTN:, :], preferred_element_type=F32))
    h = x_ref[...] + _rms(mix, gpost_ref[...])

    hn = _rms(h, gffn_ref[...]).astype(BF16)
    ff = jnp.zeros(h.shape, F32)
    for c0 in range(0, D_FF, FF_CHUNK):
        cols = slice(c0, c0 + FF_CHUNK)
        gate = jnp.dot(hn, wg_ref[:, cols], preferred_element_type=F32)
        up = jnp.dot(hn, wu_ref[:, cols], preferred_element_type=F32)
        act = (gate * jax.nn.sigmoid(gate) * up).astype(BF16)
        ff = ff + jnp.dot(act, wd_ref[cols, :], preferred_element_type=F32)
    h = h + _rms(ff, gffpost_ref[...])

    e = _rms(jnp.dot(p_ref[...].astype(BF16), wpp_ref[...], preferred_element_type=F32),
             gple_ref[...])
    gate = jax.nn.sigmoid(jnp.dot(h.astype(BF16), wpg_ref[...], preferred_element_type=F32))
    o_ref[...] = h + gate * e


def _tail_call(x2, a2, nm2, p2, gattn, wo, gpost, gffn, wg, wu, wd, gffpost, wpp, gple, wpg):
    t = x2.shape[0]
    tm = TAIL_ROWS
    rows = lambda width: pl.BlockSpec((tm, width), lambda i: (i, 0))
    const = lambda *shape: pl.BlockSpec(shape, lambda i: (0,) * len(shape),
                                        pipeline_mode=pl.Buffered(1))
    return pl.pallas_call(
        _tail_kernel,
        grid=(t // tm,),
        in_specs=[
            rows(D_MODEL), rows(D_ATTN), rows(D_POOL), rows(D_PLE),
            const(1, D_ATTN), const(D_MODEL, D_MODEL), const(1, D_MODEL), const(1, D_MODEL),
            const(D_MODEL, D_FF), const(D_MODEL, D_FF), const(D_FF, D_MODEL), const(1, D_MODEL),
            const(D_PLE, D_MODEL), const(1, D_MODEL), const(D_MODEL, D_MODEL),
        ],
        out_specs=rows(D_MODEL),
        out_shape=jax.ShapeDtypeStruct((t, D_MODEL), F32),
        compiler_params=pltpu.CompilerParams(
            dimension_semantics=("arbitrary",),
            vmem_limit_bytes=VMEM_LIMIT_BYTES),
        name="tail",
    )(x2, a2, nm2, p2, gattn, wo, gpost, gffn, wg, wu, wd, gffpost, wpp, gple, wpg)


def _head_sum_matrix():
    hsum = np.zeros((2 * D_ATTN, LANES), np.float32)
    for d in range(2 * D_ATTN):
        hsum[d, d // HEAD_DIM] = 1.0
    return jnp.asarray(hsum, BF16)


def kernel(x, p, g_mix_pre, w_in, b_forget, g_attn_grp, g_pool_grp, w_pool, pool_scale, w_out,
           g_mix_post, g_ffn_pre, w_ffn_gate, w_ffn_up, w_ffn_down, g_ffn_post, w_ple_proj,
           g_ple, w_ple_gate):
    b, s, _ = x.shape
    depth = w_in.shape[0]
    assert s % PROJ_ROWS == 0 and PROJ_ROWS == ATTN_Q_ROWS and (b * s) % TAIL_ROWS == 0
    assert (s // ATTN_Q_ROWS * ATTN_HEADS) % ITEMS_PER_TRIP == 0
    hsum = _head_sum_matrix()
    row = lambda v: v.reshape(1, -1).astype(F32)
    h = x
    for i in range(depth):
        w = w_in[i]
        o_f = 3 * D_ATTN
        o_u = o_f + ATTN_HEADS
        wf = w[:, o_f:o_u]
        w1 = jnp.concatenate(
            [wf, wf, wf, jnp.zeros((D_MODEL, LANES - PIECE_LANES), F32), w[:, o_u:],
             w[:, :D_ATTN] * (1.0 / math.sqrt(HEAD_DIM)), w[:, D_ATTN:o_f]],
            axis=1).astype(BF16)
        bf = b_forget[i].astype(F32)
        bf3 = jnp.concatenate([bf, bf, bf, jnp.zeros((LANES - PIECE_LANES,), F32)])
        wp = w_pool[i].astype(BF16)
        zero = jnp.zeros((POOL_CH, POOL_CH), BF16)
        wpool2 = jnp.stack([jnp.block([[wp[2 * j], zero], [zero, wp[2 * j + 1]]])
                            for j in range(POOL_GROUPS // 2)])
        q, k, v, nm, stats = _proj_call(
            h, row(g_mix_pre[i]), w1, bf3.reshape(1, LANES), hsum,
            wpool2, row(pool_scale[i]), row(g_pool_grp[i]))
        a = _attn_call(q, k, v, stats)
        t = b * s
        h = _tail_call(
            h.reshape(t, D_MODEL), a.reshape(t, D_ATTN), nm.reshape(t, D_POOL),
            p[i].reshape(t, D_PLE), row(g_attn_grp[i]), w_out[i].astype(BF16),
            row(g_mix_post[i]), row(g_ffn_pre[i]), w_ffn_gate[i].astype(BF16),
            w_ffn_up[i].astype(BF16), w_ffn_down[i].astype(BF16), row(g_ffn_post[i]),
            w_ple_proj[i].astype(BF16), row(g_ple[i]), w_ple_gate[i].astype(BF16),
        ).reshape(b, s, D_MODEL)
    return h
```

```python
import math

import jax
import jax.numpy as jnp
import numpy as np
from jax import lax
from jax.experimental import pallas as pl
from jax.experimental.pallas import tpu as pltpu

D_MODEL = 1024
D_PLE = 256
ATTN_HEADS = 8
HEAD_DIM = 64
D_ATTN = ATTN_HEADS * HEAD_DIM
POOL_WINDOWS = (2, 4, 8, 16)
POOL_GROUPS = len(POOL_WINDOWS)
POOL_CH = 128
D_POOL = POOL_GROUPS * POOL_CH
D_FF = 2816
RMS_EPS = 1e-6
LOG2E = math.log2(math.e)

LANES = 128
HALO = 16
CUMSUM_CHUNK = 256
N_PIECES = 3
PIECE_LANES = N_PIECES * ATTN_HEADS
U_COL = LANES
QKV_COL = U_COL + D_POOL
W1_COLS = QKV_COL + 3 * D_ATTN

PROJ_ROWS = 512
ATTN_Q_ROWS = 512
ITEMS_PER_TRIP = 8
ITEMS_PER_SHORT_TRIP = 4
STATS_ROWS = 8
SKIP_LOG2 = 160.0
NORM_SLACK = 1.01
TAIL_ROWS = 512
FF_CHUNK = 256
VMEM_LIMIT_BYTES = 56 * 1024 * 1024

BF16 = jnp.bfloat16
F32 = jnp.float32


def _aug_lane(head):
    return HEAD_DIM if head % 2 == 0 else 0


def _rms(x, g):
    ms = jnp.mean(x * x, axis=-1, keepdims=True)
    return x * lax.rsqrt(ms + RMS_EPS) * g


def _split3(x):
    hi = x.astype(BF16).astype(F32)
    r1 = x - hi
    mid = r1.astype(BF16).astype(F32)
    lo = (r1 - mid).astype(BF16).astype(F32)
    return hi, mid, lo


def _by_piece(lane, first, second, third):
    return jnp.where(lane < ATTN_HEADS, first, jnp.where(lane < 2 * ATTN_HEADS, second, third))


def _proj_kernel(x_ref, g_ref, w1_ref, bf_ref, hsum_ref, wpool_ref, pscale_ref, gpool_ref,
                 q_ref, k_ref, v_ref, nm_ref, stats_ref, carry_ref, halo_ref):
    tile = pl.program_id(1)
    tm = x_ref.shape[1]

    @pl.when(tile == 0)
    def _():
        carry_ref[...] = jnp.zeros_like(carry_ref)
        halo_ref[...] = jnp.zeros_like(halo_ref)

    hn = _rms(x_ref[0], g_ref[...]).astype(BF16)
    z = jnp.dot(hn, w1_ref[...], preferred_element_type=F32)
    lane = lax.broadcasted_iota(jnp.int32, (tm, LANES), 1)

    zf = z[:, :U_COL] + bf_ref[...]
    logf = jnp.minimum(zf, 0.0) - jnp.log1p(jnp.exp(-jnp.abs(zf)))
    logf = jnp.where(lane < PIECE_LANES, logf, 0.0)
    pieces = _by_piece(lane, *_split3(logf)).astype(BF16)
    r = lax.broadcasted_iota(jnp.int32, (CUMSUM_CHUNK, CUMSUM_CHUNK), 0)
    c = lax.broadcasted_iota(jnp.int32, (CUMSUM_CHUNK, CUMSUM_CHUNK), 1)
    tri = jnp.where(c <= r, 1.0, 0.0).astype(BF16)
    carry = carry_ref[0:1, :]
    chunks = []
    for r0 in range(0, tm, CUMSUM_CHUNK):
        cs = carry + jnp.dot(tri, pieces[r0:r0 + CUMSUM_CHUNK], preferred_element_type=F32)
        carry = cs[CUMSUM_CHUNK - 1:CUMSUM_CHUNK, :]
        chunks.append(cs)
    carry_ref[0:1, :] = carry
    part = jnp.concatenate(chunks, axis=0)
    total = (part + pltpu.roll(part, LANES - ATTN_HEADS, axis=1)
             + pltpu.roll(part, LANES - 2 * ATTN_HEADS, axis=1))
    negc = -LOG2E * total
    negc3 = _by_piece(lane, negc, pltpu.roll(negc, ATTN_HEADS, axis=1),
                      pltpu.roll(negc, 2 * ATTN_HEADS, axis=1))
    cp_odd = jnp.where(lane < PIECE_LANES, _by_piece(lane, *_split3(negc3)), 0.0)
    cp_even = pltpu.roll(cp_odd, HEAD_DIM, axis=1)

    for h in range(ATTN_HEADS):
        first = QKV_COL + (h // 2) * LANES
        slab = lambda section: z[:, section * D_ATTN + first:section * D_ATTN + first + LANES]
        a0 = _aug_lane(h)
        data = (lane < HEAD_DIM) if h % 2 == 0 else (lane >= HEAD_DIM)
        own = ((lane == a0 + h) | (lane == a0 + ATTN_HEADS + h)
               | (lane == a0 + 2 * ATTN_HEADS + h))
        q_ref[0, h] = jnp.where(data, LOG2E * slab(0), jnp.where(own, 1.0, 0.0)).astype(BF16)
        k_ref[0, h] = jnp.where(data, slab(1), cp_even if h % 2 == 0 else cp_odd).astype(BF16)
        v_ref[0, h] = jnp.where(data, slab(2), jnp.where(lane == a0, 1.0, 0.0)).astype(BF16)

    zqk = z[:, QKV_COL:QKV_COL + 2 * D_ATTN]
    sqn = jnp.dot((zqk * zqk).astype(BF16), hsum_ref[...], preferred_element_type=F32)
    stats_ref[0, 0, 0:1, :] = negc[0:1, :]
    stats_ref[0, 0, 1:2, :] = negc[tm - 1:tm, :]
    stats_ref[0, 0, 2:3, :] = jnp.max(sqn, axis=0, keepdims=True)
    stats_ref[0, 0, 3:8, :] = jnp.zeros((5, LANES), F32)

    u = z[:, U_COL:QKV_COL]
    ue = jnp.concatenate([halo_ref[...], u], axis=0)
    halo_ref[...] = u[tm - HALO:, :]
    pos = tile * tm + lax.broadcasted_iota(jnp.int32, (tm, POOL_CH), 0)
    ys = []
    for g, w in enumerate(POOL_WINDOWS):
        ch = slice(g * POOL_CH, (g + 1) * POOL_CH)
        s = ue[:, ch]
        shift = 1
        while shift < w:
            s = s + pltpu.roll(s, shift, axis=0)
            shift *= 2
        inv_count = 1.0 / jnp.minimum(pos + 1, w).astype(F32)
        ys.append((s[HALO:] * inv_count - u[:, ch]).astype(BF16))
    ms = []
    for pair in range(POOL_GROUPS // 2):
        cols = slice(2 * pair * POOL_CH, (2 * pair + 2) * POOL_CH)
        m2 = jnp.dot(jnp.concatenate(ys[2 * pair:2 * pair + 2], axis=1), wpool_ref[pair],
                     preferred_element_type=F32) * pscale_ref[:, cols]
        ms += [m2[:, :POOL_CH], m2[:, POOL_CH:]]
    msq = sum(jnp.sum(m * m, axis=-1, keepdims=True) for m in ms) * (1.0 / D_POOL)
    rstd = lax.rsqrt(msq + RMS_EPS)
    for g in range(POOL_GROUPS):
        ch = slice(g * POOL_CH, (g + 1) * POOL_CH)
        nm_ref[0, :, ch] = (ms[g] * rstd * gpool_ref[:, ch]).astype(BF16)


def _proj_call(x, g_pre, w1, bf3, hsum, wpool2, pscale, gpool):
    b, s, _ = x.shape
    tm = PROJ_ROWS
    const = lambda *shape: pl.BlockSpec(shape, lambda bi, i: (0,) * len(shape))
    head_spec = pl.BlockSpec((1, ATTN_HEADS, tm, LANES), lambda bi, i: (bi, 0, i, 0))
    head_shape = jax.ShapeDtypeStruct((b, ATTN_HEADS, s, LANES), BF16)
    return pl.pallas_call(
        _proj_kernel,
        grid=(b, s // tm),
        in_specs=[
            pl.BlockSpec((1, tm, D_MODEL), lambda bi, i: (bi, i, 0)),
            const(1, D_MODEL),
            const(D_MODEL, W1_COLS),
            const(1, LANES),
            const(2 * D_ATTN, LANES),
            const(POOL_GROUPS // 2, 2 * POOL_CH, 2 * POOL_CH),
            const(1, D_POOL),
            const(1, D_POOL),
        ],
        out_specs=[head_spec, head_spec, head_spec,
                   pl.BlockSpec((1, tm, D_POOL), lambda bi, i: (bi, i, 0)),
                   pl.BlockSpec((1, 1, STATS_ROWS, LANES), lambda bi, i: (bi, i, 0, 0))],
        out_shape=[head_shape, head_shape, head_shape,
                   jax.ShapeDtypeStruct((b, s, D_POOL), BF16),
                   jax.ShapeDtypeStruct((b, s // tm, STATS_ROWS, LANES), F32)],
        scratch_shapes=[pltpu.VMEM((8, LANES), F32), pltpu.VMEM((HALO, D_POOL), F32)],
        compiler_params=pltpu.CompilerParams(
            dimension_semantics=("arbitrary", "arbitrary"),
            vmem_limit_bytes=VMEM_LIMIT_BYTES),
        name="proj",
    )(x, g_pre, w1, bf3, hsum, wpool2, pscale, gpool)


def _attn_kernel(items_ref, ntrips_ref, q_ref, k_ref, v_ref, o_ref, m_ref, acc_ref):
    step_id = pl.program_id(0) * pl.num_programs(1) + pl.program_id(1)
    i = pl.program_id(1)
    nh = q_ref.shape[1]
    tq = q_ref.shape[2]
    n_codes = pl.num_programs(1) * nh
    nt = (((1,), (1,)), ((), ()))
    lane = lax.broadcasted_iota(jnp.int32, (tq, LANES), 1)

    m_ref[...] = jnp.full(m_ref.shape, -jnp.inf, F32)

    @pl.when(step_id == 0)
    def _():
        acc_ref[...] = jnp.zeros(acc_ref.shape, F32)

    def qk(item):
        head, _, q_rows, k_rows, _ = item
        return lax.dot_general(q_ref[0, head, q_rows, :], k_ref[0, head, k_rows, :], nt,
                               preferred_element_type=F32)

    def run(items, ahead):
        scores = [qk(item) for item in items[:ahead]]
        for n, (head, slot, q_rows, k_rows, mask) in enumerate(items):
            s = scores.pop(0)
            if n + ahead < len(items):
                scores.append(qk(items[n + ahead]))
            if mask is not None:
                s = jnp.where(mask, s, -jnp.inf)
            m = m_ref[slot, q_rows, :]
            m_new = jnp.maximum(m, jnp.max(s, axis=-1, keepdims=True))
            p = jnp.exp2(s - jnp.concatenate([m_new] * (s.shape[1] // LANES), axis=1))
            alpha = jnp.exp2(m - m_new)
            acc_ref[slot, q_rows, :] = alpha * acc_ref[slot, q_rows, :] + jnp.dot(
                p.astype(BF16), v_ref[0, head, k_rows, :], preferred_element_type=F32)
            m_ref[slot, q_rows, :] = m_new

    all_rows = slice(0, tq)
    first_item = step_id * n_codes

    def scheduled(start, count):
        items = []
        for n in range(count):
            code = items_ref[start + n]
            valid = code < n_codes
            head = jnp.where(valid, code & (nh - 1), 0)
            block = jnp.where(valid, lax.shift_right_logical(code, nh.bit_length() - 1), 0)
            items.append((head, jnp.where(valid, head, nh), all_rows,
                          pl.ds(pl.multiple_of(block * tq, tq), tq), None))
        run(items, ahead=2)

    n_long = ntrips_ref[2 * step_id]

    @pl.loop(0, n_long)
    def _(trip):
        scheduled(first_item + trip * ITEMS_PER_TRIP, ITEMS_PER_TRIP)

    @pl.loop(0, ntrips_ref[2 * step_id + 1])
    def _(trip):
        scheduled(first_item + n_long * ITEMS_PER_TRIP, ITEMS_PER_SHORT_TRIP)

    r = lax.broadcasted_iota(jnp.int32, (tq, tq), 0)
    c = lax.broadcasted_iota(jnp.int32, (tq, tq), 1)
    own_keys = pl.ds(pl.multiple_of(i * tq, tq), tq)
    run([(hh, hh, all_rows, own_keys, c <= r) for hh in range(nh)], ahead=2)
    for pair in range(nh // 2):
        outs = []
        for hh in (2 * pair, 2 * pair + 1):
            acc = acc_ref[hh]
            l_lane = _aug_lane(hh)
            outs.append(acc / acc[:, l_lane:l_lane + 1])
        o_ref[0, :, pair * LANES:(pair + 1) * LANES] = jnp.where(lane < HEAD_DIM, outs[0], outs[1])


def _attention_schedule(stats):
    nb = stats.shape[1]
    nh = ATTN_HEADS
    neg_first = stats[:, :, 0, :nh]
    neg_last = stats[:, :, 1, :nh]
    qn = jnp.sqrt(stats[:, :, 2, :nh]) * (LOG2E * NORM_SLACK)
    kn = jnp.sqrt(stats[:, :, 2, nh:2 * nh]) * NORM_SLACK
    top = neg_last[:, None, :, :] + qn[:, :, None, :] * kn[:, None, :, :]
    floor = neg_first[:, :, None, :] - (qn * kn)[:, :, None, :]
    tile_i = lax.broadcasted_iota(jnp.int32, (1, nb, nb, nh), 1)
    block_j = lax.broadcasted_iota(jnp.int32, (1, nb, nb, nh), 2)
    head = lax.broadcasted_iota(jnp.int32, (1, nb, nb, nh), 3)
    need = (block_j < tile_i) & ~(top - floor <= -SKIP_LOG2)
    n_codes = nb * nh
    codes = jnp.where(need, block_j * nh + head, n_codes).reshape(stats.shape[0], nb, n_codes)
    items = jnp.sort(codes, axis=-1).astype(jnp.int32)
    count = jnp.sum(need, axis=(2, 3)).astype(jnp.int32)
    rest = count % ITEMS_PER_TRIP
    n_long = count // ITEMS_PER_TRIP + (rest > ITEMS_PER_SHORT_TRIP)
    n_short = (rest > 0) & (rest <= ITEMS_PER_SHORT_TRIP)
    ntrips = jnp.stack([n_long, n_short.astype(jnp.int32)], axis=-1)
    return items.reshape(-1), ntrips.reshape(-1)


def _attn_call(q, k, v, stats):
    b, nh, s, _ = q.shape
    tq = ATTN_Q_ROWS
    items, ntrips = _attention_schedule(stats)
    kv_spec = pl.BlockSpec((1, nh, s, LANES), lambda bi, i, *_: (bi, 0, 0, 0))
    return pl.pallas_call(
        _attn_kernel,
        grid_spec=pltpu.PrefetchScalarGridSpec(
            num_scalar_prefetch=2,
            grid=(b, s // tq),
            in_specs=[pl.BlockSpec((1, nh, tq, LANES), lambda bi, i, *_: (bi, 0, i, 0)),
                      kv_spec, kv_spec],
            out_specs=pl.BlockSpec((1, tq, nh * HEAD_DIM), lambda bi, i, *_: (bi, i, 0)),
            scratch_shapes=[pltpu.VMEM((nh + 1, tq, LANES), F32)] * 2,
        ),
        out_shape=jax.ShapeDtypeStruct((b, s, D_ATTN), F32),
        compiler_params=pltpu.CompilerParams(
            dimension_semantics=("arbitrary", "arbitrary"),
            vmem_limit_bytes=VMEM_LIMIT_BYTES),
        name="fox_attn",
    )(items, ntrips, q, k, v)


def _tail_kernel(x_ref, a_ref, nm_ref, p_ref, gattn_ref, wo_ref, gpost_ref, gffn_ref,
                 wg_ref, wu_ref, wd_ref, gffpost_ref, wpp_ref, gple_ref, wpg_ref, o_ref):
    tm = x_ref.shape[0]
    halves = (slice(0, tm // 2), slice(tm // 2, tm))

    def swiglu(hn, cols):
        gate = jnp.dot(hn, wg_ref[:, cols], preferred_element_type=F32)
        up = jnp.dot(hn, wu_ref[:, cols], preferred_element_type=F32)
        act = (gate * jax.nn.sigmoid(gate) * up).astype(BF16)
        return jnp.dot(act, wd_ref[cols, :], preferred_element_type=F32)

    na = _rms(a_ref[...], gattn_ref[...]).astype(BF16)
    mixes = [jnp.dot(na[rows], wo_ref[:D_ATTN, :], preferred_element_type=F32)
             + jnp.dot(nm_ref[rows, :], wo_ref[D_ATTN:, :], preferred_element_type=F32)
             for rows in halves]
    hs, hns, ffs = [], [], []
    for rows, mix in zip(halves, mixes):
        h = x_ref[rows, :] + _rms(mix, gpost_ref[...])
        hn = _rms(h, gffn_ref[...]).astype(BF16)
        hs.append(h)
        hns.append(hn)
        ffs.append(swiglu(hn, slice(0, FF_CHUNK)))
    hn = jnp.concatenate(hns, axis=0)
    ff = jnp.concatenate(ffs, axis=0)
    for c0 in range(FF_CHUNK, D_FF, FF_CHUNK):
        ff = ff + swiglu(hn, slice(c0, c0 + FF_CHUNK))

    e = _rms(jnp.dot(p_ref[...].astype(BF16), wpp_ref[...], preferred_element_type=F32),
             gple_ref[...])
    for rows, h in zip(halves, hs):
        h = h + _rms(ff[rows], gffpost_ref[...])
        gate = jax.nn.sigmoid(jnp.dot(h.astype(BF16), wpg_ref[...], preferred_element_type=F32))
        o_ref[rows, :] = h + gate * e[rows]


def _tail_call(x2, a2, nm2, p2, gattn, wo, gpost, gffn, wg, wu, wd, gffpost, wpp, gple, wpg):
    t = x2.shape[0]
    tm = TAIL_ROWS
    rows = lambda width: pl.BlockSpec((tm, width), lambda i: (i, 0))
    const = lambda *shape: pl.BlockSpec(shape, lambda i: (0,) * len(shape),
                                        pipeline_mode=pl.Buffered(1))
    return pl.pallas_call(
        _tail_kernel,
        grid=(t // tm,),
        in_specs=[
            rows(D_MODEL), rows(D_ATTN), rows(D_POOL), rows(D_PLE),
            const(1, D_ATTN), const(D_MODEL, D_MODEL), const(1, D_MODEL), const(1, D_MODEL),
            const(D_MODEL, D_FF), const(D_MODEL, D_FF), const(D_FF, D_MODEL), const(1, D_MODEL),
            const(D_PLE, D_MODEL), const(1, D_MODEL), const(D_MODEL, D_MODEL),
        ],
        out_specs=rows(D_MODEL),
        out_shape=jax.ShapeDtypeStruct((t, D_MODEL), F32),
        compiler_params=pltpu.CompilerParams(
            dimension_semantics=("arbitrary",),
            vmem_limit_bytes=VMEM_LIMIT_BYTES),
        name="tail",
    )(x2, a2, nm2, p2, gattn, wo, gpost, gffn, wg, wu, wd, gffpost, wpp, gple, wpg)


def _head_sum_matrix():
    hsum = np.zeros((2 * D_ATTN, LANES), np.float32)
    for d in range(2 * D_ATTN):
        hsum[d, d // HEAD_DIM] = 1.0
    return jnp.asarray(hsum, BF16)


def kernel(x, p, g_mix_pre, w_in, b_forget, g_attn_grp, g_pool_grp, w_pool, pool_scale, w_out,
           g_mix_post, g_ffn_pre, w_ffn_gate, w_ffn_up, w_ffn_down, g_ffn_post, w_ple_proj,
           g_ple, w_ple_gate):
    b, s, _ = x.shape
    depth = w_in.shape[0]
    assert s % PROJ_ROWS == 0 and PROJ_ROWS == ATTN_Q_ROWS and (b * s) % TAIL_ROWS == 0
    assert (s // ATTN_Q_ROWS * ATTN_HEADS) % ITEMS_PER_TRIP == 0
    hsum = _head_sum_matrix()
    row = lambda v: v.reshape(1, -1).astype(F32)
    h = x
    for i in range(depth):
        w = w_in[i]
        o_f = 3 * D_ATTN
        o_u = o_f + ATTN_HEADS
        wf = w[:, o_f:o_u]
        w1 = jnp.concatenate(
            [wf, wf, wf, jnp.zeros((D_MODEL, LANES - PIECE_LANES), F32), w[:, o_u:],
             w[:, :D_ATTN] * (1.0 / math.sqrt(HEAD_DIM)), w[:, D_ATTN:o_f]],
            axis=1).astype(BF16)
        bf = b_forget[i].astype(F32)
        bf3 = jnp.concatenate([bf, bf, bf, jnp.zeros((LANES - PIECE_LANES,), F32)])
        wp = w_pool[i].astype(BF16)
        zero = jnp.zeros((POOL_CH, POOL_CH), BF16)
        wpool2 = jnp.stack([jnp.block([[wp[2 * j], zero], [zero, wp[2 * j + 1]]])
                            for j in range(POOL_GROUPS // 2)])
        q, k, v, nm, stats = _proj_call(
            h, row(g_mix_pre[i]), w1, bf3.reshape(1, LANES), hsum,
            wpool2, row(pool_scale[i]), row(g_pool_grp[i]))
        a = _attn_call(q, k, v, stats)
        t = b * s
        h = _tail_call(
            h.reshape(t, D_MODEL), a.reshape(t, D_ATTN), nm.reshape(t, D_POOL),
            p[i].reshape(t, D_PLE), row(g_attn_grp[i]), w_out[i].astype(BF16),
            row(g_mix_post[i]), row(g_ffn_pre[i]), w_ffn_gate[i].astype(BF16),
            w_ffn_up[i].astype(BF16), w_ffn_down[i].astype(BF16), row(g_ffn_post[i]),
            w_ple_proj[i].astype(BF16), row(g_ple[i]), w_ple_gate[i].astype(BF16),
        ).reshape(b, s, D_MODEL)
    return h
```

```python
import functools
import math

import jax
import jax.numpy as jnp
import numpy as np
from jax import lax
from jax.experimental import pallas as pl
from jax.experimental.pallas import tpu as pltpu

D_MODEL = 1024
D_PLE = 256
ATTN_HEADS = 8
HEAD_DIM = 64
D_ATTN = ATTN_HEADS * HEAD_DIM
POOL_WINDOWS = (2, 4, 8, 16)
POOL_GROUPS = len(POOL_WINDOWS)
POOL_CH = 128
D_POOL = POOL_GROUPS * POOL_CH
D_FF = 2816
RMS_EPS = 1e-6
LOG2E = math.log2(math.e)

LANES = 128
BF16_SUBLANES = 16
HALO = 16
CUMSUM_CHUNK = 256
N_PIECES = 3
PIECE_LANES = N_PIECES * ATTN_HEADS
U_COL = LANES
QKV_COL = U_COL + D_POOL
W1_COLS = QKV_COL + 3 * D_ATTN

PROJ_ROWS = 512
ATTN_Q_ROWS = 512
ITEMS_PER_TRIP = 8
ITEMS_PER_SHORT_TRIP = 4
STATS_ROWS = 8
SKIP_LOG2 = 160.0
NORM_SLACK = 1.01
TAIL_ROWS = 512
FF_CHUNK = 256
VMEM_LIMIT_BYTES = 56 * 1024 * 1024

BF16 = jnp.bfloat16
F32 = jnp.float32


def _aug_lane(head):
    return HEAD_DIM if head % 2 == 0 else 0


def _rms(x, g):
    ms = jnp.mean(x * x, axis=-1, keepdims=True)
    return x * lax.rsqrt(ms + RMS_EPS) * g


def _split3(x):
    hi = x.astype(BF16).astype(F32)
    r1 = x - hi
    mid = r1.astype(BF16).astype(F32)
    lo = (r1 - mid).astype(BF16).astype(F32)
    return hi, mid, lo


def _by_piece(lane, first, second, third):
    return jnp.where(lane < ATTN_HEADS, first, jnp.where(lane < 2 * ATTN_HEADS, second, third))


def _proj_kernel(x_ref, g_ref, w1_ref, bf_ref, hsum_ref, wpool_ref, pscale_ref, gpool_ref,
                 q_ref, k_ref, v_ref, nm_ref, stats_ref, carry_ref, halo_ref):
    tile = pl.program_id(1)
    tm = x_ref.shape[1]

    @pl.when(tile == 0)
    def _():
        carry_ref[...] = jnp.zeros_like(carry_ref)
        halo_ref[...] = jnp.zeros_like(halo_ref)

    hn = _rms(x_ref[0], g_ref[...]).astype(BF16)
    z = jnp.dot(hn, w1_ref[...], preferred_element_type=F32)
    lane = lax.broadcasted_iota(jnp.int32, (tm, LANES), 1)

    zf = z[:, :U_COL] + bf_ref[...]
    logf = jnp.minimum(zf, 0.0) - jnp.log1p(jnp.exp(-jnp.abs(zf)))
    logf = jnp.where(lane < PIECE_LANES, logf, 0.0)
    pieces = _by_piece(lane, *_split3(logf)).astype(BF16)
    r = lax.broadcasted_iota(jnp.int32, (CUMSUM_CHUNK, CUMSUM_CHUNK), 0)
    c = lax.broadcasted_iota(jnp.int32, (CUMSUM_CHUNK, CUMSUM_CHUNK), 1)
    tri = jnp.where(c <= r, 1.0, 0.0).astype(BF16)
    carry = carry_ref[0:1, :]
    chunks = []
    for r0 in range(0, tm, CUMSUM_CHUNK):
        cs = carry + jnp.dot(tri, pieces[r0:r0 + CUMSUM_CHUNK], preferred_element_type=F32)
        carry = cs[CUMSUM_CHUNK - 1:CUMSUM_CHUNK, :]
        chunks.append(cs)
    carry_ref[0:1, :] = carry
    part = jnp.concatenate(chunks, axis=0)
    total = (part + pltpu.roll(part, LANES - ATTN_HEADS, axis=1)
             + pltpu.roll(part, LANES - 2 * ATTN_HEADS, axis=1))
    negc = -LOG2E * total
    negc3 = _by_piece(lane, negc, pltpu.roll(negc, ATTN_HEADS, axis=1),
                      pltpu.roll(negc, 2 * ATTN_HEADS, axis=1))
    cp_odd = jnp.where(lane < PIECE_LANES, _by_piece(lane, *_split3(negc3)), 0.0)
    cp_even = pltpu.roll(cp_odd, HEAD_DIM, axis=1)

    for h in range(ATTN_HEADS):
        first = QKV_COL + (h // 2) * LANES
        slab = lambda section: z[:, section * D_ATTN + first:section * D_ATTN + first + LANES]
        a0 = _aug_lane(h)
        data = (lane < HEAD_DIM) if h % 2 == 0 else (lane >= HEAD_DIM)
        own = ((lane == a0 + h) | (lane == a0 + ATTN_HEADS + h)
               | (lane == a0 + 2 * ATTN_HEADS + h))
        q_ref[0, h] = jnp.where(data, LOG2E * slab(0), jnp.where(own, 1.0, 0.0)).astype(BF16)
        k_ref[0, h] = jnp.where(data, slab(1), cp_even if h % 2 == 0 else cp_odd).astype(BF16)
        v_ref[0, h] = jnp.where(data, slab(2), jnp.where(lane == a0, 1.0, 0.0)).astype(BF16)

    zqk = z[:, QKV_COL:QKV_COL + 2 * D_ATTN]
    sqn = jnp.dot((zqk * zqk).astype(BF16), hsum_ref[...], preferred_element_type=F32)
    stats_ref[0, 0, 0:1, :] = negc[0:1, :]
    stats_ref[0, 0, 1:2, :] = negc[tm - 1:tm, :]
    stats_ref[0, 0, 2:3, :] = jnp.max(sqn, axis=0, keepdims=True)
    stats_ref[0, 0, 3:8, :] = jnp.zeros((5, LANES), F32)

    u = z[:, U_COL:QKV_COL]
    ue = jnp.concatenate([halo_ref[...], u], axis=0)
    halo_ref[...] = u[tm - HALO:, :]
    pos = tile * tm + lax.broadcasted_iota(jnp.int32, (tm, POOL_CH), 0)
    ys = []
    for g, w in enumerate(POOL_WINDOWS):
        ch = slice(g * POOL_CH, (g + 1) * POOL_CH)
        s = ue[:, ch]
        shift = 1
        while shift < w:
            s = s + pltpu.roll(s, shift, axis=0)
            shift *= 2
        inv_count = 1.0 / jnp.minimum(pos + 1, w).astype(F32)
        ys.append((s[HALO:] * inv_count - u[:, ch]).astype(BF16))
    ms = []
    for pair in range(POOL_GROUPS // 2):
        cols = slice(2 * pair * POOL_CH, (2 * pair + 2) * POOL_CH)
        m2 = jnp.dot(jnp.concatenate(ys[2 * pair:2 * pair + 2], axis=1), wpool_ref[pair],
                     preferred_element_type=F32) * pscale_ref[:, cols]
        ms += [m2[:, :POOL_CH], m2[:, POOL_CH:]]
    msq = sum(jnp.sum(m * m, axis=-1, keepdims=True) for m in ms) * (1.0 / D_POOL)
    rstd = lax.rsqrt(msq + RMS_EPS)
    for g in range(POOL_GROUPS):
        ch = slice(g * POOL_CH, (g + 1) * POOL_CH)
        nm_ref[0, :, ch] = (ms[g] * rstd * gpool_ref[:, ch]).astype(BF16)


def _proj_call(x, g_pre, w1, bf3, hsum, wpool2, pscale, gpool):
    b, s, _ = x.shape
    tm = PROJ_ROWS
    const = lambda *shape: pl.BlockSpec(shape, lambda bi, i: (0,) * len(shape))
    head_spec = pl.BlockSpec((1, ATTN_HEADS, tm, LANES), lambda bi, i: (bi, 0, i, 0))
    head_shape = jax.ShapeDtypeStruct((b, ATTN_HEADS, s, LANES), BF16)
    return pl.pallas_call(
        _proj_kernel,
        grid=(b, s // tm),
        in_specs=[
            pl.BlockSpec((1, tm, D_MODEL), lambda bi, i: (bi, i, 0)),
            const(1, D_MODEL),
            const(D_MODEL, W1_COLS),
            const(1, LANES),
            const(2 * D_ATTN, LANES),
            const(POOL_GROUPS // 2, 2 * POOL_CH, 2 * POOL_CH),
            const(1, D_POOL),
            const(1, D_POOL),
        ],
        out_specs=[head_spec, head_spec, head_spec,
                   pl.BlockSpec((1, tm, D_POOL), lambda bi, i: (bi, i, 0)),
                   pl.BlockSpec((1, 1, STATS_ROWS, LANES), lambda bi, i: (bi, i, 0, 0))],
        out_shape=[head_shape, head_shape, head_shape,
                   jax.ShapeDtypeStruct((b, s, D_POOL), BF16),
                   jax.ShapeDtypeStruct((b, s // tm, STATS_ROWS, LANES), F32)],
        scratch_shapes=[pltpu.VMEM((8, LANES), F32), pltpu.VMEM((HALO, D_POOL), F32)],
        compiler_params=pltpu.CompilerParams(
            dimension_semantics=("arbitrary", "arbitrary"),
            vmem_limit_bytes=VMEM_LIMIT_BYTES),
        name="proj",
    )(x, g_pre, w1, bf3, hsum, wpool2, pscale, gpool)


def _attn_kernel(n_cast, items_ref, ntrips_ref, q_ref, k_ref, v_ref, *refs):
    cast_in, o_ref, cast_out = refs[:n_cast], refs[n_cast], refs[n_cast + 1:2 * n_cast + 1]
    m_ref, acc_ref = refs[2 * n_cast + 1:]
    step_id = pl.program_id(0) * pl.num_programs(1) + pl.program_id(1)
    i = pl.program_id(1)
    nh = q_ref.shape[1]
    tq = q_ref.shape[2]
    n_codes = pl.num_programs(1) * nh
    nt = (((1,), (1,)), ((), ()))
    lane = lax.broadcasted_iota(jnp.int32, (tq, LANES), 1)

    for src, dst in zip(cast_in, cast_out):
        dst[...] = src[...].astype(BF16)

    m_ref[...] = jnp.full(m_ref.shape, -jnp.inf, F32)

    @pl.when(step_id == 0)
    def _():
        acc_ref[...] = jnp.zeros(acc_ref.shape, F32)

    def qk(item):
        head, _, q_rows, k_rows, _ = item
        return lax.dot_general(q_ref[0, head, q_rows, :], k_ref[0, head, k_rows, :], nt,
                               preferred_element_type=F32)

    def run(items, ahead):
        scores = [qk(item) for item in items[:ahead]]
        for n, (head, slot, q_rows, k_rows, mask) in enumerate(items):
            s = scores.pop(0)
            if n + ahead < len(items):
                scores.append(qk(items[n + ahead]))
            if mask is not None:
                s = jnp.where(mask, s, -jnp.inf)
            m = m_ref[slot, q_rows, :]
            m_new = jnp.maximum(m, jnp.max(s, axis=-1, keepdims=True))
            p = jnp.exp2(s - jnp.concatenate([m_new] * (s.shape[1] // LANES), axis=1))
            alpha = jnp.exp2(m - m_new)
            acc_ref[slot, q_rows, :] = alpha * acc_ref[slot, q_rows, :] + jnp.dot(
                p.astype(BF16), v_ref[0, head, k_rows, :], preferred_element_type=F32)
            m_ref[slot, q_rows, :] = m_new

    all_rows = slice(0, tq)
    first_item = step_id * n_codes

    def scheduled(start, count):
        items = []
        for n in range(count):
            code = items_ref[start + n]
            valid = code < n_codes
            head = jnp.where(valid, code & (nh - 1), 0)
            block = jnp.where(valid, lax.shift_right_logical(code, nh.bit_length() - 1), 0)
            items.append((head, jnp.where(valid, head, nh), all_rows,
                          pl.ds(pl.multiple_of(block * tq, tq), tq), None))
        run(items, ahead=2)

    n_long = ntrips_ref[2 * step_id]

    @pl.loop(0, n_long)
    def _(trip):
        scheduled(first_item + trip * ITEMS_PER_TRIP, ITEMS_PER_TRIP)

    @pl.loop(0, ntrips_ref[2 * step_id + 1])
    def _(trip):
        scheduled(first_item + n_long * ITEMS_PER_TRIP, ITEMS_PER_SHORT_TRIP)

    r = lax.broadcasted_iota(jnp.int32, (tq, tq), 0)
    c = lax.broadcasted_iota(jnp.int32, (tq, tq), 1)
    own_keys = pl.ds(pl.multiple_of(i * tq, tq), tq)
    run([(hh, hh, all_rows, own_keys, c <= r) for hh in range(nh)], ahead=2)
    for pair in range(nh // 2):
        outs = []
        for hh in (2 * pair, 2 * pair + 1):
            acc = acc_ref[hh]
            l_lane = _aug_lane(hh)
            outs.append(acc / acc[:, l_lane:l_lane + 1])
        o_ref[0, :, pair * LANES:(pair + 1) * LANES] = jnp.where(lane < HEAD_DIM, outs[0], outs[1])


def _attention_schedule(stats):
    nb = stats.shape[1]
    nh = ATTN_HEADS
    neg_first = stats[:, :, 0, :nh]
    neg_last = stats[:, :, 1, :nh]
    qn = jnp.sqrt(stats[:, :, 2, :nh]) * (LOG2E * NORM_SLACK)
    kn = jnp.sqrt(stats[:, :, 2, nh:2 * nh]) * NORM_SLACK
    top = neg_last[:, None, :, :] + qn[:, :, None, :] * kn[:, None, :, :]
    floor = neg_first[:, :, None, :] - (qn * kn)[:, :, None, :]
    tile_i = lax.broadcasted_iota(jnp.int32, (1, nb, nb, nh), 1)
    block_j = lax.broadcasted_iota(jnp.int32, (1, nb, nb, nh), 2)
    head = lax.broadcasted_iota(jnp.int32, (1, nb, nb, nh), 3)
    need = (block_j < tile_i) & ~(top - floor <= -SKIP_LOG2)
    n_codes = nb * nh
    codes = jnp.where(need, block_j * nh + head, n_codes).reshape(stats.shape[0], nb, n_codes)
    items = jnp.sort(codes, axis=-1).astype(jnp.int32)
    count = jnp.sum(need, axis=(2, 3)).astype(jnp.int32)
    rest = count % ITEMS_PER_TRIP
    n_long = count // ITEMS_PER_TRIP + (rest > ITEMS_PER_SHORT_TRIP)
    n_short = (rest > 0) & (rest <= ITEMS_PER_SHORT_TRIP)
    ntrips = jnp.stack([n_long, n_short.astype(jnp.int32)], axis=-1)
    return items.reshape(-1), ntrips.reshape(-1)


def _cast_block_rows(rows, steps):
    for block in range(BF16_SUBLANES, rows + 1, BF16_SUBLANES):
        if rows % block == 0 and rows // block <= steps:
            return block
    raise ValueError(f"no bf16 row block for {rows} rows in {steps} steps")


def _attn_call(q, k, v, stats, cast_weights):
    b, nh, s, _ = q.shape
    tq = ATTN_Q_ROWS
    nq = s // tq
    items, ntrips = _attention_schedule(stats)
    kv_spec = pl.BlockSpec((1, nh, s, LANES), lambda bi, i, *_: (bi, 0, 0, 0))
    cast_specs = []
    for w in cast_weights:
        rows, cols = w.shape
        block = _cast_block_rows(rows, b * nq)
        last = rows // block - 1
        cast_specs.append(pl.BlockSpec(
            (block, cols), lambda bi, i, *_, last=last: (jnp.minimum(bi * nq + i, last), 0)))
    outs = pl.pallas_call(
        functools.partial(_attn_kernel, len(cast_weights)),
        grid_spec=pltpu.PrefetchScalarGridSpec(
            num_scalar_prefetch=2,
            grid=(b, nq),
            in_specs=[pl.BlockSpec((1, nh, tq, LANES), lambda bi, i, *_: (bi, 0, i, 0)),
                      kv_spec, kv_spec] + cast_specs,
            out_specs=[pl.BlockSpec((1, tq, nh * HEAD_DIM), lambda bi, i, *_: (bi, i, 0))]
            + cast_specs,
            scratch_shapes=[pltpu.VMEM((nh + 1, tq, LANES), F32)] * 2,
        ),
        out_shape=[jax.ShapeDtypeStruct((b, s, D_ATTN), F32)]
        + [jax.ShapeDtypeStruct(w.shape, BF16) for w in cast_weights],
        compiler_params=pltpu.CompilerParams(
            dimension_semantics=("arbitrary", "arbitrary"),
            vmem_limit_bytes=VMEM_LIMIT_BYTES),
        name="fox_attn",
    )(items, ntrips, q, k, v, *cast_weights)
    return outs[0], outs[1:]


def _tail_kernel(x_ref, a_ref, nm_ref, p_ref, gattn_ref, wo_ref, gpost_ref, gffn_ref,
                 wg_ref, wu_ref, wd_ref, gffpost_ref, wpp_ref, gple_ref, wpg_ref, o_ref):
    tm = x_ref.shape[0]
    halves = (slice(0, tm // 2), slice(tm // 2, tm))

    def swiglu(hn, cols):
        gate = jnp.dot(hn, wg_ref[:, cols], preferred_element_type=F32)
        up = jnp.dot(hn, wu_ref[:, cols], preferred_element_type=F32)
        act = (gate * jax.nn.sigmoid(gate) * up).astype(BF16)
        return jnp.dot(act, wd_ref[cols, :], preferred_element_type=F32)

    na = _rms(a_ref[...], gattn_ref[...]).astype(BF16)
    mixes = [jnp.dot(na[rows], wo_ref[:D_ATTN, :], preferred_element_type=F32)
             + jnp.dot(nm_ref[rows, :], wo_ref[D_ATTN:, :], preferred_element_type=F32)
             for rows in halves]
    hs, hns, ffs = [], [], []
    for rows, mix in zip(halves, mixes):
        h = x_ref[rows, :] + _rms(mix, gpost_ref[...])
        hn = _rms(h, gffn_ref[...]).astype(BF16)
        hs.append(h)
        hns.append(hn)
        ffs.append(swiglu(hn, slice(0, FF_CHUNK)))
    hn = jnp.concatenate(hns, axis=0)
    ff = jnp.concatenate(ffs, axis=0)
    for c0 in range(FF_CHUNK, D_FF, FF_CHUNK):
        ff = ff + swiglu(hn, slice(c0, c0 + FF_CHUNK))

    e = _rms(jnp.dot(p_ref[...].astype(BF16), wpp_ref[...], preferred_element_type=F32),
             gple_ref[...])
    for rows, h in zip(halves, hs):
        h = h + _rms(ff[rows], gffpost_ref[...])
        gate = jax.nn.sigmoid(jnp.dot(h.astype(BF16), wpg_ref[...], preferred_element_type=F32))
        o_ref[rows, :] = h + gate * e[rows]


def _tail_call(x2, a2, nm2, p2, gattn, wo, gpost, gffn, wg, wu, wd, gffpost, wpp, gple, wpg):
    t = x2.shape[0]
    tm = TAIL_ROWS
    rows = lambda width: pl.BlockSpec((tm, width), lambda i: (i, 0))
    const = lambda *shape: pl.BlockSpec(shape, lambda i: (0,) * len(shape),
                                        pipeline_mode=pl.Buffered(1))
    return pl.pallas_call(
        _tail_kernel,
        grid=(t // tm,),
        in_specs=[
            rows(D_MODEL), rows(D_ATTN), rows(D_POOL), rows(D_PLE),
            const(1, D_ATTN), const(D_MODEL, D_MODEL), const(1, D_MODEL), const(1, D_MODEL),
            const(D_MODEL, D_FF), const(D_MODEL, D_FF), const(D_FF, D_MODEL), const(1, D_MODEL),
            const(D_PLE, D_MODEL), const(1, D_MODEL), const(D_MODEL, D_MODEL),
        ],
        out_specs=rows(D_MODEL),
        out_shape=jax.ShapeDtypeStruct((t, D_MODEL), F32),
        compiler_params=pltpu.CompilerParams(
            dimension_semantics=("arbitrary",),
            vmem_limit_bytes=VMEM_LIMIT_BYTES),
        name="tail",
    )(x2, a2, nm2, p2, gattn, wo, gpost, gffn, wg, wu, wd, gffpost, wpp, gple, wpg)


def _head_sum_matrix():
    hsum = np.zeros((2 * D_ATTN, LANES), np.float32)
    for d in range(2 * D_ATTN):
        hsum[d, d // HEAD_DIM] = 1.0
    return jnp.asarray(hsum, BF16)


def kernel(x, p, g_mix_pre, w_in, b_forget, g_attn_grp, g_pool_grp, w_pool, pool_scale, w_out,
           g_mix_post, g_ffn_pre, w_ffn_gate, w_ffn_up, w_ffn_down, g_ffn_post, w_ple_proj,
           g_ple, w_ple_gate):
    b, s, _ = x.shape
    depth = w_in.shape[0]
    assert s % PROJ_ROWS == 0 and PROJ_ROWS == ATTN_Q_ROWS and (b * s) % TAIL_ROWS == 0
    assert (s // ATTN_Q_ROWS * ATTN_HEADS) % ITEMS_PER_TRIP == 0
    hsum = _head_sum_matrix()
    row = lambda v: v.reshape(1, -1).astype(F32)
    h = x
    for i in range(depth):
        w = w_in[i]
        o_f = 3 * D_ATTN
        o_u = o_f + ATTN_HEADS
        wf = w[:, o_f:o_u]
        w1 = jnp.concatenate(
            [wf, wf, wf, jnp.zeros((D_MODEL, LANES - PIECE_LANES), F32), w[:, o_u:],
             w[:, :D_ATTN] * (1.0 / math.sqrt(HEAD_DIM)), w[:, D_ATTN:o_f]],
            axis=1).astype(BF16)
        bf = b_forget[i].astype(F32)
        bf3 = jnp.concatenate([bf, bf, bf, jnp.zeros((LANES - PIECE_LANES,), F32)])
        wp = w_pool[i].astype(BF16)
        zero = jnp.zeros((POOL_CH, POOL_CH), BF16)
        wpool2 = jnp.stack([jnp.block([[wp[2 * j], zero], [zero, wp[2 * j + 1]]])
                            for j in range(POOL_GROUPS // 2)])
        q, k, v, nm, stats = _proj_call(
            h, row(g_mix_pre[i]), w1, bf3.reshape(1, LANES), hsum,
            wpool2, row(pool_scale[i]), row(g_pool_grp[i]))
        a, (wo, wg, wu, wd, wpp, wpg) = _attn_call(
            q, k, v, stats, [w_out[i], w_ffn_gate[i], w_ffn_up[i], w_ffn_down[i],
                             w_ple_proj[i], w_ple_gate[i]])
        t = b * s
        h = _tail_call(
            h.reshape(t, D_MODEL), a.reshape(t, D_ATTN), nm.reshape(t, D_POOL),
            p[i].reshape(t, D_PLE), row(g_attn_grp[i]), wo, row(g_mix_post[i]),
            row(g_ffn_pre[i]), wg, wu, wd, row(g_ffn_post[i]), wpp, row(g_ple[i]), wpg,
        ).reshape(b, s, D_MODEL)
    return h
```

```python
import functools
import math

import jax
import jax.numpy as jnp
import numpy as np
from jax import lax
from jax.experimental import pallas as pl
from jax.experimental.pallas import tpu as pltpu

D_MODEL = 1024
D_PLE = 256
ATTN_HEADS = 8
HEAD_DIM = 64
D_ATTN = ATTN_HEADS * HEAD_DIM
POOL_WINDOWS = (2, 4, 8, 16)
POOL_GROUPS = len(POOL_WINDOWS)
POOL_CH = 128
D_POOL = POOL_GROUPS * POOL_CH
D_FF = 2816
RMS_EPS = 1e-6
LOG2E = math.log2(math.e)

LANES = 128
BF16_SUBLANES = 16
HALO = 16
CUMSUM_CHUNK = 256
N_PIECES = 3
PIECE_LANES = N_PIECES * ATTN_HEADS
U_COL = LANES
QKV_COL = U_COL + D_POOL
W1_COLS = QKV_COL + 3 * D_ATTN

PROJ_ROWS = 512
ATTN_Q_ROWS = 512
ITEMS_PER_TRIP = 8
ITEMS_PER_SHORT_TRIP = 4
STATS_ROWS = 8
SKIP_LOG2 = 160.0
NORM_SLACK = 1.01
TAIL_ROWS = 1024
FF_CHUNK = 256
VMEM_LIMIT_BYTES = 59 * 1024 * 1024

BF16 = jnp.bfloat16
F32 = jnp.float32


def _aug_lane(head):
    return HEAD_DIM if head % 2 == 0 else 0


def _rms(x, g):
    ms = jnp.mean(x * x, axis=-1, keepdims=True)
    return x * lax.rsqrt(ms + RMS_EPS) * g


def _split3(x):
    hi = x.astype(BF16).astype(F32)
    r1 = x - hi
    mid = r1.astype(BF16).astype(F32)
    lo = (r1 - mid).astype(BF16).astype(F32)
    return hi, mid, lo


def _by_piece(lane, first, second, third):
    return jnp.where(lane < ATTN_HEADS, first, jnp.where(lane < 2 * ATTN_HEADS, second, third))


def _proj_kernel(x_ref, g_ref, w1_ref, bf_ref, hsum_ref, wpool_ref, pscale_ref, gpool_ref,
                 q_ref, k_ref, v_ref, nm_ref, stats_ref, carry_ref, halo_ref):
    tile = pl.program_id(1)
    tm = x_ref.shape[1]

    @pl.when(tile == 0)
    def _():
        carry_ref[...] = jnp.zeros_like(carry_ref)
        halo_ref[...] = jnp.zeros_like(halo_ref)

    hn = _rms(x_ref[0], g_ref[...]).astype(BF16)
    z = jnp.dot(hn, w1_ref[...], preferred_element_type=F32)
    lane = lax.broadcasted_iota(jnp.int32, (tm, LANES), 1)

    zf = z[:, :U_COL] + bf_ref[...]
    logf = jnp.minimum(zf, 0.0) - jnp.log1p(jnp.exp(-jnp.abs(zf)))
    logf = jnp.where(lane < PIECE_LANES, logf, 0.0)
    pieces = _by_piece(lane, *_split3(logf)).astype(BF16)
    r = lax.broadcasted_iota(jnp.int32, (CUMSUM_CHUNK, CUMSUM_CHUNK), 0)
    c = lax.broadcasted_iota(jnp.int32, (CUMSUM_CHUNK, CUMSUM_CHUNK), 1)
    tri = jnp.where(c <= r, 1.0, 0.0).astype(BF16)
    carry = carry_ref[0:1, :]
    chunks = []
    for r0 in range(0, tm, CUMSUM_CHUNK):
        cs = carry + jnp.dot(tri, pieces[r0:r0 + CUMSUM_CHUNK], preferred_element_type=F32)
        carry = cs[CUMSUM_CHUNK - 1:CUMSUM_CHUNK, :]
        chunks.append(cs)
    carry_ref[0:1, :] = carry
    part = jnp.concatenate(chunks, axis=0)
    total = (part + pltpu.roll(part, LANES - ATTN_HEADS, axis=1)
             + pltpu.roll(part, LANES - 2 * ATTN_HEADS, axis=1))
    negc = -LOG2E * total
    negc3 = _by_piece(lane, negc, pltpu.roll(negc, ATTN_HEADS, axis=1),
                      pltpu.roll(negc, 2 * ATTN_HEADS, axis=1))
    cp_odd = jnp.where(lane < PIECE_LANES, _by_piece(lane, *_split3(negc3)), 0.0)
    cp_even = pltpu.roll(cp_odd, HEAD_DIM, axis=1)

    for h in range(ATTN_HEADS):
        first = QKV_COL + (h // 2) * LANES
        slab = lambda section: z[:, section * D_ATTN + first:section * D_ATTN + first + LANES]
        a0 = _aug_lane(h)
        data = (lane < HEAD_DIM) if h % 2 == 0 else (lane >= HEAD_DIM)
        own = ((lane == a0 + h) | (lane == a0 + ATTN_HEADS + h)
               | (lane == a0 + 2 * ATTN_HEADS + h))
        q_ref[0, h] = jnp.where(data, LOG2E * slab(0), jnp.where(own, 1.0, 0.0)).astype(BF16)
        k_ref[0, h] = jnp.where(data, slab(1), cp_even if h % 2 == 0 else cp_odd).astype(BF16)
        v_ref[0, h] = jnp.where(data, slab(2), jnp.where(lane == a0, 1.0, 0.0)).astype(BF16)

    zqk = z[:, QKV_COL:QKV_COL + 2 * D_ATTN]
    sqn = jnp.dot((zqk * zqk).astype(BF16), hsum_ref[...], preferred_element_type=F32)
    stats_ref[0, 0, 0:1, :] = negc[0:1, :]
    stats_ref[0, 0, 1:2, :] = negc[tm - 1:tm, :]
    stats_ref[0, 0, 2:3, :] = jnp.max(sqn, axis=0, keepdims=True)
    stats_ref[0, 0, 3:8, :] = jnp.zeros((5, LANES), F32)

    u = z[:, U_COL:QKV_COL]
    ue = jnp.concatenate([halo_ref[...], u], axis=0)
    halo_ref[...] = u[tm - HALO:, :]
    pos = tile * tm + lax.broadcasted_iota(jnp.int32, (tm, POOL_CH), 0)
    ys = []
    for g, w in enumerate(POOL_WINDOWS):
        ch = slice(g * POOL_CH, (g + 1) * POOL_CH)
        s = ue[:, ch]
        shift = 1
        while shift < w:
            s = s + pltpu.roll(s, shift, axis=0)
            shift *= 2
        inv_count = 1.0 / jnp.minimum(pos + 1, w).astype(F32)
        ys.append((s[HALO:] * inv_count - u[:, ch]).astype(BF16))
    ms = []
    for pair in range(POOL_GROUPS // 2):
        cols = slice(2 * pair * POOL_CH, (2 * pair + 2) * POOL_CH)
        m2 = jnp.dot(jnp.concatenate(ys[2 * pair:2 * pair + 2], axis=1), wpool_ref[pair],
                     preferred_element_type=F32) * pscale_ref[:, cols]
        ms += [m2[:, :POOL_CH], m2[:, POOL_CH:]]
    msq = sum(jnp.sum(m * m, axis=-1, keepdims=True) for m in ms) * (1.0 / D_POOL)
    rstd = lax.rsqrt(msq + RMS_EPS)
    for g in range(POOL_GROUPS):
        ch = slice(g * POOL_CH, (g + 1) * POOL_CH)
        nm_ref[0, :, ch] = (ms[g] * rstd * gpool_ref[:, ch]).astype(BF16)


def _proj_call(x, g_pre, w1, bf3, hsum, wpool2, pscale, gpool):
    b, s, _ = x.shape
    tm = PROJ_ROWS
    const = lambda *shape: pl.BlockSpec(shape, lambda bi, i: (0,) * len(shape))
    head_spec = pl.BlockSpec((1, ATTN_HEADS, tm, LANES), lambda bi, i: (bi, 0, i, 0))
    head_shape = jax.ShapeDtypeStruct((b, ATTN_HEADS, s, LANES), BF16)
    return pl.pallas_call(
        _proj_kernel,
        grid=(b, s // tm),
        in_specs=[
            pl.BlockSpec((1, tm, D_MODEL), lambda bi, i: (bi, i, 0)),
            const(1, D_MODEL),
            const(D_MODEL, W1_COLS),
            const(1, LANES),
            const(2 * D_ATTN, LANES),
            const(POOL_GROUPS // 2, 2 * POOL_CH, 2 * POOL_CH),
            const(1, D_POOL),
            const(1, D_POOL),
        ],
        out_specs=[head_spec, head_spec, head_spec,
                   pl.BlockSpec((1, tm, D_POOL), lambda bi, i: (bi, i, 0)),
                   pl.BlockSpec((1, 1, STATS_ROWS, LANES), lambda bi, i: (bi, i, 0, 0))],
        out_shape=[head_shape, head_shape, head_shape,
                   jax.ShapeDtypeStruct((b, s, D_POOL), BF16),
                   jax.ShapeDtypeStruct((b, s // tm, STATS_ROWS, LANES), F32)],
        scratch_shapes=[pltpu.VMEM((8, LANES), F32), pltpu.VMEM((HALO, D_POOL), F32)],
        compiler_params=pltpu.CompilerParams(
            dimension_semantics=("arbitrary", "arbitrary"),
            vmem_limit_bytes=VMEM_LIMIT_BYTES),
        name="proj",
    )(x, g_pre, w1, bf3, hsum, wpool2, pscale, gpool)


def _attn_kernel(n_cast, items_ref, ntrips_ref, q_ref, k_ref, v_ref, *refs):
    cast_in, o_ref, cast_out = refs[:n_cast], refs[n_cast], refs[n_cast + 1:2 * n_cast + 1]
    m_ref, acc_ref = refs[2 * n_cast + 1:]
    step_id = pl.program_id(0) * pl.num_programs(1) + pl.program_id(1)
    i = pl.program_id(1)
    nh = q_ref.shape[1]
    tq = q_ref.shape[2]
    n_codes = pl.num_programs(1) * nh
    nt = (((1,), (1,)), ((), ()))
    lane = lax.broadcasted_iota(jnp.int32, (tq, LANES), 1)

    for src, dst in zip(cast_in, cast_out):
        dst[...] = src[...].astype(BF16)

    m_ref[...] = jnp.full(m_ref.shape, -jnp.inf, F32)

    @pl.when(step_id == 0)
    def _():
        acc_ref[...] = jnp.zeros(acc_ref.shape, F32)

    def qk(item):
        head, _, q_rows, k_rows, _ = item
        return lax.dot_general(q_ref[0, head, q_rows, :], k_ref[0, head, k_rows, :], nt,
                               preferred_element_type=F32)

    def run(items, ahead):
        scores = [qk(item) for item in items[:ahead]]
        for n, (head, slot, q_rows, k_rows, mask) in enumerate(items):
            s = scores.pop(0)
            if n + ahead < len(items):
                scores.append(qk(items[n + ahead]))
            if mask is not None:
                s = jnp.where(mask, s, -jnp.inf)
            m = m_ref[slot, q_rows, :]
            m_new = jnp.maximum(m, jnp.max(s, axis=-1, keepdims=True))
            p = jnp.exp2(s - jnp.concatenate([m_new] * (s.shape[1] // LANES), axis=1))
            alpha = jnp.exp2(m - m_new)
            acc_ref[slot, q_rows, :] = alpha * acc_ref[slot, q_rows, :] + jnp.dot(
                p.astype(BF16), v_ref[0, head, k_rows, :], preferred_element_type=F32)
            m_ref[slot, q_rows, :] = m_new

    all_rows = slice(0, tq)
    first_item = step_id * n_codes

    def scheduled(start, count):
        items = []
        for n in range(count):
            code = items_ref[start + n]
            valid = code < n_codes
            head = jnp.where(valid, code & (nh - 1), 0)
            block = jnp.where(valid, lax.shift_right_logical(code, nh.bit_length() - 1), 0)
            items.append((head, jnp.where(valid, head, nh), all_rows,
                          pl.ds(pl.multiple_of(block * tq, tq), tq), None))
        run(items, ahead=2)

    n_long = ntrips_ref[2 * step_id]

    @pl.loop(0, n_long)
    def _(trip):
        scheduled(first_item + trip * ITEMS_PER_TRIP, ITEMS_PER_TRIP)

    @pl.loop(0, ntrips_ref[2 * step_id + 1])
    def _(trip):
        scheduled(first_item + n_long * ITEMS_PER_TRIP, ITEMS_PER_SHORT_TRIP)

    r = lax.broadcasted_iota(jnp.int32, (tq, tq), 0)
    c = lax.broadcasted_iota(jnp.int32, (tq, tq), 1)
    own_keys = pl.ds(pl.multiple_of(i * tq, tq), tq)
    run([(hh, hh, all_rows, own_keys, c <= r) for hh in range(nh)], ahead=2)
    for pair in range(nh // 2):
        outs = []
        for hh in (2 * pair, 2 * pair + 1):
            acc = acc_ref[hh]
            l_lane = _aug_lane(hh)
            outs.append(acc / acc[:, l_lane:l_lane + 1])
        o_ref[0, :, pair * LANES:(pair + 1) * LANES] = jnp.where(lane < HEAD_DIM, outs[0], outs[1])


def _attention_schedule(stats):
    nb = stats.shape[1]
    nh = ATTN_HEADS
    neg_first = stats[:, :, 0, :nh]
    neg_last = stats[:, :, 1, :nh]
    qn = jnp.sqrt(stats[:, :, 2, :nh]) * (LOG2E * NORM_SLACK)
    kn = jnp.sqrt(stats[:, :, 2, nh:2 * nh]) * NORM_SLACK
    top = neg_last[:, None, :, :] + qn[:, :, None, :] * kn[:, None, :, :]
    floor = neg_first[:, :, None, :] - (qn * kn)[:, :, None, :]
    tile_i = lax.broadcasted_iota(jnp.int32, (1, nb, nb, nh), 1)
    block_j = lax.broadcasted_iota(jnp.int32, (1, nb, nb, nh), 2)
    head = lax.broadcasted_iota(jnp.int32, (1, nb, nb, nh), 3)
    need = (block_j < tile_i) & ~(top - floor <= -SKIP_LOG2)
    n_codes = nb * nh
    codes = jnp.where(need, block_j * nh + head, n_codes).reshape(stats.shape[0], nb, n_codes)
    items = jnp.sort(codes, axis=-1).astype(jnp.int32)
    count = jnp.sum(need, axis=(2, 3)).astype(jnp.int32)
    rest = count % ITEMS_PER_TRIP
    n_long = count // ITEMS_PER_TRIP + (rest > ITEMS_PER_SHORT_TRIP)
    n_short = (rest > 0) & (rest <= ITEMS_PER_SHORT_TRIP)
    ntrips = jnp.stack([n_long, n_short.astype(jnp.int32)], axis=-1)
    return items.reshape(-1), ntrips.reshape(-1)


def _cast_block_rows(rows, steps):
    for block in range(BF16_SUBLANES, rows + 1, BF16_SUBLANES):
        if rows % block == 0 and rows // block <= steps:
            return block
    raise ValueError(f"no bf16 row block for {rows} rows in {steps} steps")


def _attn_call(q, k, v, stats, cast_weights):
    b, nh, s, _ = q.shape
    tq = ATTN_Q_ROWS
    nq = s // tq
    items, ntrips = _attention_schedule(stats)
    kv_spec = pl.BlockSpec((1, nh, s, LANES), lambda bi, i, *_: (bi, 0, 0, 0))
    cast_specs = []
    for w in cast_weights:
        rows, cols = w.shape
        block = _cast_block_rows(rows, b * nq)
        last = rows // block - 1
        cast_specs.append(pl.BlockSpec(
            (block, cols), lambda bi, i, *_, last=last: (jnp.minimum(bi * nq + i, last), 0)))
    outs = pl.pallas_call(
        functools.partial(_attn_kernel, len(cast_weights)),
        grid_spec=pltpu.PrefetchScalarGridSpec(
            num_scalar_prefetch=2,
            grid=(b, nq),
            in_specs=[pl.BlockSpec((1, nh, tq, LANES), lambda bi, i, *_: (bi, 0, i, 0)),
                      kv_spec, kv_spec] + cast_specs,
            out_specs=[pl.BlockSpec((1, tq, nh * HEAD_DIM), lambda bi, i, *_: (bi, i, 0))]
            + cast_specs,
            scratch_shapes=[pltpu.VMEM((nh + 1, tq, LANES), F32)] * 2,
        ),
        out_shape=[jax.ShapeDtypeStruct((b, s, D_ATTN), F32)]
        + [jax.ShapeDtypeStruct(w.shape, BF16) for w in cast_weights],
        compiler_params=pltpu.CompilerParams(
            dimension_semantics=("arbitrary", "arbitrary"),
            vmem_limit_bytes=VMEM_LIMIT_BYTES),
        name="fox_attn",
    )(items, ntrips, q, k, v, *cast_weights)
    return outs[0], outs[1:]


def _tail_kernel(x_ref, a_ref, nm_ref, p_ref, gattn_ref, wo_ref, gpost_ref, gffn_ref,
                 wg_ref, wu_ref, wd_ref, gffpost_ref, wpp_ref, gple_ref, wpg_ref, o_ref):
    tm = x_ref.shape[0]
    halves = (slice(0, tm // 2), slice(tm // 2, tm))

    def swiglu(hn, cols):
        gate = jnp.dot(hn, wg_ref[:, cols], preferred_element_type=F32)
        up = jnp.dot(hn, wu_ref[:, cols], preferred_element_type=F32)
        act = (gate * jax.nn.sigmoid(gate) * up).astype(BF16)
        return jnp.dot(act, wd_ref[cols, :], preferred_element_type=F32)

    na = _rms(a_ref[...], gattn_ref[...]).astype(BF16)
    mixes = [jnp.dot(na[rows], wo_ref[:D_ATTN, :], preferred_element_type=F32)
             + jnp.dot(nm_ref[rows, :], wo_ref[D_ATTN:, :], preferred_element_type=F32)
             for rows in halves]
    hs, hns, ffs = [], [], []
    for rows, mix in zip(halves, mixes):
        h = x_ref[rows, :] + _rms(mix, gpost_ref[...])
        hn = _rms(h, gffn_ref[...]).astype(BF16)
        hs.append(h)
        hns.append(hn)
        ffs.append(swiglu(hn, slice(0, FF_CHUNK)))
    hn = jnp.concatenate(hns, axis=0)
    ff = jnp.concatenate(ffs, axis=0)
    for c0 in range(FF_CHUNK, D_FF, FF_CHUNK):
        ff = ff + swiglu(hn, slice(c0, c0 + FF_CHUNK))

    e = _rms(jnp.dot(p_ref[...].astype(BF16), wpp_ref[...], preferred_element_type=F32),
             gple_ref[...])
    for rows, h in zip(halves, hs):
        h = h + _rms(ff[rows], gffpost_ref[...])
        gate = jax.nn.sigmoid(jnp.dot(h.astype(BF16), wpg_ref[...], preferred_element_type=F32))
        o_ref[rows, :] = h + gate * e[rows]


def _tail_call(x2, a2, nm2, p2, gattn, wo, gpost, gffn, wg, wu, wd, gffpost, wpp, gple, wpg):
    t = x2.shape[0]
    tm = TAIL_ROWS
    rows = lambda width: pl.BlockSpec((tm, width), lambda i: (i, 0))
    const = lambda *shape: pl.BlockSpec(shape, lambda i: (0,) * len(shape),
                                        pipeline_mode=pl.Buffered(1))
    return pl.pallas_call(
        _tail_kernel,
        grid=(t // tm,),
        in_specs=[
            rows(D_MODEL), rows(D_ATTN), rows(D_POOL), rows(D_PLE),
            const(1, D_ATTN), const(D_MODEL, D_MODEL), const(1, D_MODEL), const(1, D_MODEL),
            const(D_MODEL, D_FF), const(D_MODEL, D_FF), const(D_FF, D_MODEL), const(1, D_MODEL),
            const(D_PLE, D_MODEL), const(1, D_MODEL), const(D_MODEL, D_MODEL),
        ],
        out_specs=rows(D_MODEL),
        out_shape=jax.ShapeDtypeStruct((t, D_MODEL), F32),
        compiler_params=pltpu.CompilerParams(
            dimension_semantics=("arbitrary",),
            vmem_limit_bytes=VMEM_LIMIT_BYTES),
        name="tail",
    )(x2, a2, nm2, p2, gattn, wo, gpost, gffn, wg, wu, wd, gffpost, wpp, gple, wpg)


def _head_sum_matrix():
    hsum = np.zeros((2 * D_ATTN, LANES), np.float32)
    for d in range(2 * D_ATTN):
        hsum[d, d // HEAD_DIM] = 1.0
    return jnp.asarray(hsum, BF16)


def kernel(x, p, g_mix_pre, w_in, b_forget, g_attn_grp, g_pool_grp, w_pool, pool_scale, w_out,
           g_mix_post, g_ffn_pre, w_ffn_gate, w_ffn_up, w_ffn_down, g_ffn_post, w_ple_proj,
           g_ple, w_ple_gate):
    b, s, _ = x.shape
    depth = w_in.shape[0]
    assert s % PROJ_ROWS == 0 and PROJ_ROWS == ATTN_Q_ROWS and (b * s) % TAIL_ROWS == 0
    assert (s // ATTN_Q_ROWS * ATTN_HEADS) % ITEMS_PER_TRIP == 0
    hsum = _head_sum_matrix()
    row = lambda v: v.reshape(1, -1).astype(F32)
    h = x
    for i in range(depth):
        w = w_in[i]
        o_f = 3 * D_ATTN
        o_u = o_f + ATTN_HEADS
        wf = w[:, o_f:o_u]
        w1 = jnp.concatenate(
            [wf, wf, wf, jnp.zeros((D_MODEL, LANES - PIECE_LANES), F32), w[:, o_u:],
             w[:, :D_ATTN] * (1.0 / math.sqrt(HEAD_DIM)), w[:, D_ATTN:o_f]],
            axis=1).astype(BF16)
        bf = b_forget[i].astype(F32)
        bf3 = jnp.concatenate([bf, bf, bf, jnp.zeros((LANES - PIECE_LANES,), F32)])
        wp = w_pool[i].astype(BF16)
        zero = jnp.zeros((POOL_CH, POOL_CH), BF16)
        wpool2 = jnp.stack([jnp.block([[wp[2 * j], zero], [zero, wp[2 * j + 1]]])
                            for j in range(POOL_GROUPS // 2)])
        q, k, v, nm, stats = _proj_call(
            h, row(g_mix_pre[i]), w1, bf3.reshape(1, LANES), hsum,
            wpool2, row(pool_scale[i]), row(g_pool_grp[i]))
        a, (wo, wg, wu, wd, wpp, wpg) = _attn_call(
            q, k, v, stats, [w_out[i], w_ffn_gate[i], w_ffn_up[i], w_ffn_down[i],
                             w_ple_proj[i], w_ple_gate[i]])
        t = b * s
        h = _tail_call(
            h.reshape(t, D_MODEL), a.reshape(t, D_ATTN), nm.reshape(t, D_POOL),
            p[i].reshape(t, D_PLE), row(g_attn_grp[i]), wo, row(g_mix_post[i]),
            row(g_ffn_pre[i]), wg, wu, wd, row(g_ffn_post[i]), wpp, row(g_ple[i]), wpg,
        ).reshape(b, s, D_MODEL)
    return h
```

```python
import functools
import math

import jax
import jax.numpy as jnp
import numpy as np
from jax import lax
from jax.experimental import pallas as pl
from jax.experimental.pallas import tpu as pltpu

D_MODEL = 1024
D_PLE = 256
ATTN_HEADS = 8
HEAD_DIM = 64
D_ATTN = ATTN_HEADS * HEAD_DIM
POOL_WINDOWS = (2, 4, 8, 16)
POOL_GROUPS = len(POOL_WINDOWS)
POOL_CH = 128
D_POOL = POOL_GROUPS * POOL_CH
D_FF = 2816
RMS_EPS = 1e-6
LOG2E = math.log2(math.e)

LANES = 128
BF16_SUBLANES = 16
HALO = 16
CUMSUM_CHUNK = 256
N_PIECES = 3
PIECE_LANES = N_PIECES * ATTN_HEADS
U_COL = LANES
QKV_COL = U_COL + D_POOL
W1_COLS = QKV_COL + 3 * D_ATTN

PROJ_ROWS = 1024
ATTN_Q_ROWS = 512
ITEMS_PER_TRIP = 8
ITEMS_PER_SHORT_TRIP = 4
STATS_ROWS = 8
SKIP_LOG2 = 160.0
NORM_SLACK = 1.01
TAIL_ROWS = 1024
FF_CHUNK = 256
VMEM_LIMIT_BYTES = 59 * 1024 * 1024

BF16 = jnp.bfloat16
F32 = jnp.float32


def _aug_lane(head):
    return HEAD_DIM if head % 2 == 0 else 0


def _rms(x, g):
    ms = jnp.mean(x * x, axis=-1, keepdims=True)
    return x * lax.rsqrt(ms + RMS_EPS) * g


def _split3(x):
    hi = x.astype(BF16).astype(F32)
    r1 = x - hi
    mid = r1.astype(BF16).astype(F32)
    lo = (r1 - mid).astype(BF16).astype(F32)
    return hi, mid, lo


def _by_piece(lane, first, second, third):
    return jnp.where(lane < ATTN_HEADS, first, jnp.where(lane < 2 * ATTN_HEADS, second, third))


def _proj_kernel(x_ref, g_ref, w1_ref, bf_ref, hsum_ref, wpool_ref, pscale_ref, gpool_ref,
                 q_ref, k_ref, v_ref, nm_ref, stats_ref, carry_ref, halo_ref):
    tile = pl.program_id(1)
    tm = x_ref.shape[1]

    @pl.when(tile == 0)
    def _():
        carry_ref[...] = jnp.zeros_like(carry_ref)
        halo_ref[...] = jnp.zeros_like(halo_ref)

    hn = _rms(x_ref[0], g_ref[...]).astype(BF16)
    z = jnp.dot(hn, w1_ref[...], preferred_element_type=F32)
    lane = lax.broadcasted_iota(jnp.int32, (tm, LANES), 1)

    zf = z[:, :U_COL] + bf_ref[...]
    logf = jnp.minimum(zf, 0.0) - jnp.log1p(jnp.exp(-jnp.abs(zf)))
    logf = jnp.where(lane < PIECE_LANES, logf, 0.0)
    pieces = _by_piece(lane, *_split3(logf)).astype(BF16)
    r = lax.broadcasted_iota(jnp.int32, (CUMSUM_CHUNK, CUMSUM_CHUNK), 0)
    c = lax.broadcasted_iota(jnp.int32, (CUMSUM_CHUNK, CUMSUM_CHUNK), 1)
    tri = jnp.where(c <= r, 1.0, 0.0).astype(BF16)
    carry = carry_ref[0:1, :]
    chunks = []
    for r0 in range(0, tm, CUMSUM_CHUNK):
        cs = carry + jnp.dot(tri, pieces[r0:r0 + CUMSUM_CHUNK], preferred_element_type=F32)
        carry = cs[CUMSUM_CHUNK - 1:CUMSUM_CHUNK, :]
        chunks.append(cs)
    carry_ref[0:1, :] = carry
    part = jnp.concatenate(chunks, axis=0)
    total = (part + pltpu.roll(part, LANES - ATTN_HEADS, axis=1)
             + pltpu.roll(part, LANES - 2 * ATTN_HEADS, axis=1))
    negc = -LOG2E * total
    negc3 = _by_piece(lane, negc, pltpu.roll(negc, ATTN_HEADS, axis=1),
                      pltpu.roll(negc, 2 * ATTN_HEADS, axis=1))
    cp_odd = jnp.where(lane < PIECE_LANES, _by_piece(lane, *_split3(negc3)), 0.0)
    cp_even = pltpu.roll(cp_odd, HEAD_DIM, axis=1)

    for h in range(ATTN_HEADS):
        first = QKV_COL + (h // 2) * LANES
        slab = lambda section: z[:, section * D_ATTN + first:section * D_ATTN + first + LANES]
        a0 = _aug_lane(h)
        data = (lane < HEAD_DIM) if h % 2 == 0 else (lane >= HEAD_DIM)
        own = ((lane == a0 + h) | (lane == a0 + ATTN_HEADS + h)
               | (lane == a0 + 2 * ATTN_HEADS + h))
        q_ref[0, h] = jnp.where(data, LOG2E * slab(0), jnp.where(own, 1.0, 0.0)).astype(BF16)
        k_ref[0, h] = jnp.where(data, slab(1), cp_even if h % 2 == 0 else cp_odd).astype(BF16)
        v_ref[0, h] = jnp.where(data, slab(2), jnp.where(lane == a0, 1.0, 0.0)).astype(BF16)

    zqk = z[:, QKV_COL:QKV_COL + 2 * D_ATTN]
    sqn = jnp.dot((zqk * zqk).astype(BF16), hsum_ref[...], preferred_element_type=F32)
    for t in range(tm // ATTN_Q_ROWS):
        lo, hi = t * ATTN_Q_ROWS, (t + 1) * ATTN_Q_ROWS
        stats_ref[0, t, 0:1, :] = negc[lo:lo + 1, :]
        stats_ref[0, t, 1:2, :] = negc[hi - 1:hi, :]
        stats_ref[0, t, 2:3, :] = jnp.max(sqn[lo:hi], axis=0, keepdims=True)
        stats_ref[0, t, 3:8, :] = jnp.zeros((5, LANES), F32)

    u = z[:, U_COL:QKV_COL]
    ue = jnp.concatenate([halo_ref[...], u], axis=0)
    halo_ref[...] = u[tm - HALO:, :]
    pos = tile * tm + lax.broadcasted_iota(jnp.int32, (tm, POOL_CH), 0)
    ys = []
    for g, w in enumerate(POOL_WINDOWS):
        ch = slice(g * POOL_CH, (g + 1) * POOL_CH)
        s = ue[:, ch]
        shift = 1
        while shift < w:
            s = s + pltpu.roll(s, shift, axis=0)
            shift *= 2
        inv_count = 1.0 / jnp.minimum(pos + 1, w).astype(F32)
        ys.append((s[HALO:] * inv_count - u[:, ch]).astype(BF16))
    ms = []
    for pair in range(POOL_GROUPS // 2):
        cols = slice(2 * pair * POOL_CH, (2 * pair + 2) * POOL_CH)
        m2 = jnp.dot(jnp.concatenate(ys[2 * pair:2 * pair + 2], axis=1), wpool_ref[pair],
                     preferred_element_type=F32) * pscale_ref[:, cols]
        ms += [m2[:, :POOL_CH], m2[:, POOL_CH:]]
    msq = sum(jnp.sum(m * m, axis=-1, keepdims=True) for m in ms) * (1.0 / D_POOL)
    rstd = lax.rsqrt(msq + RMS_EPS)
    for g in range(POOL_GROUPS):
        ch = slice(g * POOL_CH, (g + 1) * POOL_CH)
        nm_ref[0, :, ch] = (ms[g] * rstd * gpool_ref[:, ch]).astype(BF16)


def _proj_call(x, g_pre, w1, bf3, hsum, wpool2, pscale, gpool):
    b, s, _ = x.shape
    tm = PROJ_ROWS
    const = lambda *shape: pl.BlockSpec(shape, lambda bi, i: (0,) * len(shape))
    head_spec = pl.BlockSpec((1, ATTN_HEADS, tm, LANES), lambda bi, i: (bi, 0, i, 0))
    head_shape = jax.ShapeDtypeStruct((b, ATTN_HEADS, s, LANES), BF16)
    return pl.pallas_call(
        _proj_kernel,
        grid=(b, s // tm),
        in_specs=[
            pl.BlockSpec((1, tm, D_MODEL), lambda bi, i: (bi, i, 0)),
            const(1, D_MODEL),
            const(D_MODEL, W1_COLS),
            const(1, LANES),
            const(2 * D_ATTN, LANES),
            const(POOL_GROUPS // 2, 2 * POOL_CH, 2 * POOL_CH),
            const(1, D_POOL),
            const(1, D_POOL),
        ],
        out_specs=[head_spec, head_spec, head_spec,
                   pl.BlockSpec((1, tm, D_POOL), lambda bi, i: (bi, i, 0)),
                   pl.BlockSpec((1, tm // ATTN_Q_ROWS, STATS_ROWS, LANES),
                                lambda bi, i: (bi, i, 0, 0))],
        out_shape=[head_shape, head_shape, head_shape,
                   jax.ShapeDtypeStruct((b, s, D_POOL), BF16),
                   jax.ShapeDtypeStruct((b, s // ATTN_Q_ROWS, STATS_ROWS, LANES), F32)],
        scratch_shapes=[pltpu.VMEM((8, LANES), F32), pltpu.VMEM((HALO, D_POOL), F32)],
        compiler_params=pltpu.CompilerParams(
            dimension_semantics=("arbitrary", "arbitrary"),
            vmem_limit_bytes=VMEM_LIMIT_BYTES),
        name="proj",
    )(x, g_pre, w1, bf3, hsum, wpool2, pscale, gpool)


def _attn_kernel(n_cast, items_ref, ntrips_ref, q_ref, k_ref, v_ref, *refs):
    cast_in, o_ref, cast_out = refs[:n_cast], refs[n_cast], refs[n_cast + 1:2 * n_cast + 1]
    m_ref, acc_ref = refs[2 * n_cast + 1:]
    step_id = pl.program_id(0) * pl.num_programs(1) + pl.program_id(1)
    i = pl.program_id(1)
    nh = q_ref.shape[1]
    tq = q_ref.shape[2]
    n_codes = pl.num_programs(1) * nh
    nt = (((1,), (1,)), ((), ()))
    lane = lax.broadcasted_iota(jnp.int32, (tq, LANES), 1)

    for src, dst in zip(cast_in, cast_out):
        dst[...] = src[...].astype(BF16)

    m_ref[...] = jnp.full(m_ref.shape, -jnp.inf, F32)

    @pl.when(step_id == 0)
    def _():
        acc_ref[...] = jnp.zeros(acc_ref.shape, F32)

    def qk(item):
        head, _, q_rows, k_rows, _ = item
        return lax.dot_general(q_ref[0, head, q_rows, :], k_ref[0, head, k_rows, :], nt,
                               preferred_element_type=F32)

    def run(items, ahead):
        scores = [qk(item) for item in items[:ahead]]
        for n, (head, slot, q_rows, k_rows, mask) in enumerate(items):
            s = scores.pop(0)
            if n + ahead < len(items):
                scores.append(qk(items[n + ahead]))
            if mask is not None:
                s = jnp.where(mask, s, -jnp.inf)
            m = m_ref[slot, q_rows, :]
            m_new = jnp.maximum(m, jnp.max(s, axis=-1, keepdims=True))
            p = jnp.exp2(s - jnp.concatenate([m_new] * (s.shape[1] // LANES), axis=1))
            alpha = jnp.exp2(m - m_new)
            acc_ref[slot, q_rows, :] = alpha * acc_ref[slot, q_rows, :] + jnp.dot(
                p.astype(BF16), v_ref[0, head, k_rows, :], preferred_element_type=F32)
            m_ref[slot, q_rows, :] = m_new

    all_rows = slice(0, tq)
    first_item = step_id * n_codes

    def scheduled(start, count):
        items = []
        for n in range(count):
            code = items_ref[start + n]
            valid = code < n_codes
            head = jnp.where(valid, code & (nh - 1), 0)
            block = jnp.where(valid, lax.shift_right_logical(code, nh.bit_length() - 1), 0)
            items.append((head, jnp.where(valid, head, nh), all_rows,
                          pl.ds(pl.multiple_of(block * tq, tq), tq), None))
        run(items, ahead=2)

    n_long = ntrips_ref[2 * step_id]

    @pl.loop(0, n_long)
    def _(trip):
        scheduled(first_item + trip * ITEMS_PER_TRIP, ITEMS_PER_TRIP)

    @pl.loop(0, ntrips_ref[2 * step_id + 1])
    def _(trip):
        scheduled(first_item + n_long * ITEMS_PER_TRIP, ITEMS_PER_SHORT_TRIP)

    r = lax.broadcasted_iota(jnp.int32, (tq, tq), 0)
    c = lax.broadcasted_iota(jnp.int32, (tq, tq), 1)
    own_keys = pl.ds(pl.multiple_of(i * tq, tq), tq)
    run([(hh, hh, all_rows, own_keys, c <= r) for hh in range(nh)], ahead=2)
    for pair in range(nh // 2):
        outs = []
        for hh in (2 * pair, 2 * pair + 1):
            acc = acc_ref[hh]
            l_lane = _aug_lane(hh)
            outs.append(acc / acc[:, l_lane:l_lane + 1])
        o_ref[0, :, pair * LANES:(pair + 1) * LANES] = jnp.where(lane < HEAD_DIM, outs[0], outs[1])


def _attention_schedule(stats):
    nb = stats.shape[1]
    nh = ATTN_HEADS
    neg_first = stats[:, :, 0, :nh]
    neg_last = stats[:, :, 1, :nh]
    qn = jnp.sqrt(stats[:, :, 2, :nh]) * (LOG2E * NORM_SLACK)
    kn = jnp.sqrt(stats[:, :, 2, nh:2 * nh]) * NORM_SLACK
    top = neg_last[:, None, :, :] + qn[:, :, None, :] * kn[:, None, :, :]
    floor = neg_first[:, :, None, :] - (qn * kn)[:, :, None, :]
    tile_i = lax.broadcasted_iota(jnp.int32, (1, nb, nb, nh), 1)
    block_j = lax.broadcasted_iota(jnp.int32, (1, nb, nb, nh), 2)
    head = lax.broadcasted_iota(jnp.int32, (1, nb, nb, nh), 3)
    need = (block_j < tile_i) & ~(top - floor <= -SKIP_LOG2)
    n_codes = nb * nh
    codes = jnp.where(need, block_j * nh + head, n_codes).reshape(stats.shape[0], nb, n_codes)
    items = jnp.sort(codes, axis=-1).astype(jnp.int32)
    count = jnp.sum(need, axis=(2, 3)).astype(jnp.int32)
    rest = count % ITEMS_PER_TRIP
    n_long = count // ITEMS_PER_TRIP + (rest > ITEMS_PER_SHORT_TRIP)
    n_short = (rest > 0) & (rest <= ITEMS_PER_SHORT_TRIP)
    ntrips = jnp.stack([n_long, n_short.astype(jnp.int32)], axis=-1)
    return items.reshape(-1), ntrips.reshape(-1)


def _cast_block_rows(rows, steps):
    for block in range(BF16_SUBLANES, rows + 1, BF16_SUBLANES):
        if rows % block == 0 and rows // block <= steps:
            return block
    raise ValueError(f"no bf16 row block for {rows} rows in {steps} steps")


def _attn_call(q, k, v, stats, cast_weights):
    b, nh, s, _ = q.shape
    tq = ATTN_Q_ROWS
    nq = s // tq
    items, ntrips = _attention_schedule(stats)
    kv_spec = pl.BlockSpec((1, nh, s, LANES), lambda bi, i, *_: (bi, 0, 0, 0))
    cast_specs = []
    for w in cast_weights:
        rows, cols = w.shape
        block = _cast_block_rows(rows, b * nq)
        last = rows // block - 1
        cast_specs.append(pl.BlockSpec(
            (block, cols), lambda bi, i, *_, last=last: (jnp.minimum(bi * nq + i, last), 0)))
    outs = pl.pallas_call(
        functools.partial(_attn_kernel, len(cast_weights)),
        grid_spec=pltpu.PrefetchScalarGridSpec(
            num_scalar_prefetch=2,
            grid=(b, nq),
            in_specs=[pl.BlockSpec((1, nh, tq, LANES), lambda bi, i, *_: (bi, 0, i, 0)),
                      kv_spec, kv_spec] + cast_specs,
            out_specs=[pl.BlockSpec((1, tq, nh * HEAD_DIM), lambda bi, i, *_: (bi, i, 0))]
            + cast_specs,
            scratch_shapes=[pltpu.VMEM((nh + 1, tq, LANES), F32)] * 2,
        ),
        out_shape=[jax.ShapeDtypeStruct((b, s, D_ATTN), F32)]
        + [jax.ShapeDtypeStruct(w.shape, BF16) for w in cast_weights],
        compiler_params=pltpu.CompilerParams(
            dimension_semantics=("arbitrary", "arbitrary"),
            vmem_limit_bytes=VMEM_LIMIT_BYTES),
        name="fox_attn",
    )(items, ntrips, q, k, v, *cast_weights)
    return outs[0], outs[1:]


def _tail_kernel(x_ref, a_ref, nm_ref, p_ref, gattn_ref, wo_ref, gpost_ref, gffn_ref,
                 wg_ref, wu_ref, wd_ref, gffpost_ref, wpp_ref, gple_ref, wpg_ref, o_ref):
    tm = x_ref.shape[0]
    halves = (slice(0, tm // 2), slice(tm // 2, tm))

    def swiglu(hn, cols):
        gate = jnp.dot(hn, wg_ref[:, cols], preferred_element_type=F32)
        up = jnp.dot(hn, wu_ref[:, cols], preferred_element_type=F32)
        act = (gate * jax.nn.sigmoid(gate) * up).astype(BF16)
        return jnp.dot(act, wd_ref[cols, :], preferred_element_type=F32)

    na = _rms(a_ref[...], gattn_ref[...]).astype(BF16)
    mixes = [jnp.dot(na[rows], wo_ref[:D_ATTN, :], preferred_element_type=F32)
             + jnp.dot(nm_ref[rows, :], wo_ref[D_ATTN:, :], preferred_element_type=F32)
             for rows in halves]
    hs, hns, ffs = [], [], []
    for rows, mix in zip(halves, mixes):
        h = x_ref[rows, :] + _rms(mix, gpost_ref[...])
        hn = _rms(h, gffn_ref[...]).astype(BF16)
        hs.append(h)
        hns.append(hn)
        ffs.append(swiglu(hn, slice(0, FF_CHUNK)))
    hn = jnp.concatenate(hns, axis=0)
    ff = jnp.concatenate(ffs, axis=0)
    for c0 in range(FF_CHUNK, D_FF, FF_CHUNK):
        ff = ff + swiglu(hn, slice(c0, c0 + FF_CHUNK))

    e = _rms(jnp.dot(p_ref[...].astype(BF16), wpp_ref[...], preferred_element_type=F32),
             gple_ref[...])
    for rows, h in zip(halves, hs):
        h = h + _rms(ff[rows], gffpost_ref[...])
        gate = jax.nn.sigmoid(jnp.dot(h.astype(BF16), wpg_ref[...], preferred_element_type=F32))
        o_ref[rows, :] = h + gate * e[rows]


def _tail_call(x2, a2, nm2, p2, gattn, wo, gpost, gffn, wg, wu, wd, gffpost, wpp, gple, wpg):
    t = x2.shape[0]
    tm = TAIL_ROWS
    rows = lambda width: pl.BlockSpec((tm, width), lambda i: (i, 0))
    const = lambda *shape: pl.BlockSpec(shape, lambda i: (0,) * len(shape),
                                        pipeline_mode=pl.Buffered(1))
    return pl.pallas_call(
        _tail_kernel,
        grid=(t // tm,),
        in_specs=[
            rows(D_MODEL), rows(D_ATTN), rows(D_POOL), rows(D_PLE),
            const(1, D_ATTN), const(D_MODEL, D_MODEL), const(1, D_MODEL), const(1, D_MODEL),
            const(D_MODEL, D_FF), const(D_MODEL, D_FF), const(D_FF, D_MODEL), const(1, D_MODEL),
            const(D_PLE, D_MODEL), const(1, D_MODEL), const(D_MODEL, D_MODEL),
        ],
        out_specs=rows(D_MODEL),
        out_shape=jax.ShapeDtypeStruct((t, D_MODEL), F32),
        compiler_params=pltpu.CompilerParams(
            dimension_semantics=("arbitrary",),
            vmem_limit_bytes=VMEM_LIMIT_BYTES),
        name="tail",
    )(x2, a2, nm2, p2, gattn, wo, gpost, gffn, wg, wu, wd, gffpost, wpp, gple, wpg)


def _head_sum_matrix():
    hsum = np.zeros((2 * D_ATTN, LANES), np.float32)
    for d in range(2 * D_ATTN):
        hsum[d, d // HEAD_DIM] = 1.0
    return jnp.asarray(hsum, BF16)


def kernel(x, p, g_mix_pre, w_in, b_forget, g_attn_grp, g_pool_grp, w_pool, pool_scale, w_out,
           g_mix_post, g_ffn_pre, w_ffn_gate, w_ffn_up, w_ffn_down, g_ffn_post, w_ple_proj,
           g_ple, w_ple_gate):
    b, s, _ = x.shape
    depth = w_in.shape[0]
    assert s % PROJ_ROWS == 0 and PROJ_ROWS % ATTN_Q_ROWS == 0 and (b * s) % TAIL_ROWS == 0
    assert (s // ATTN_Q_ROWS * ATTN_HEADS) % ITEMS_PER_TRIP == 0
    hsum = _head_sum_matrix()
    row = lambda v: v.reshape(1, -1).astype(F32)
    h = x
    for i in range(depth):
        w = w_in[i]
        o_f = 3 * D_ATTN
        o_u = o_f + ATTN_HEADS
        wf = w[:, o_f:o_u]
        w1 = jnp.concatenate(
            [wf, wf, wf, jnp.zeros((D_MODEL, LANES - PIECE_LANES), F32), w[:, o_u:],
             w[:, :D_ATTN] * (1.0 / math.sqrt(HEAD_DIM)), w[:, D_ATTN:o_f]],
            axis=1).astype(BF16)
        bf = b_forget[i].astype(F32)
        bf3 = jnp.concatenate([bf, bf, bf, jnp.zeros((LANES - PIECE_LANES,), F32)])
        wp = w_pool[i].astype(BF16)
        zero = jnp.zeros((POOL_CH, POOL_CH), BF16)
        wpool2 = jnp.stack([jnp.block([[wp[2 * j], zero], [zero, wp[2 * j + 1]]])
                            for j in range(POOL_GROUPS // 2)])
        q, k, v, nm, stats = _proj_call(
            h, row(g_mix_pre[i]), w1, bf3.reshape(1, LANES), hsum,
            wpool2, row(pool_scale[i]), row(g_pool_grp[i]))
        a, (wo, wg, wu, wd, wpp, wpg) = _attn_call(
            q, k, v, stats, [w_out[i], w_ffn_gate[i], w_ffn_up[i], w_ffn_down[i],
                             w_ple_proj[i], w_ple_gate[i]])
        t = b * s
        h = _tail_call(
            h.reshape(t, D_MODEL), a.reshape(t, D_ATTN), nm.reshape(t, D_POOL),
            p[i].reshape(t, D_PLE), row(g_attn_grp[i]), wo, row(g_mix_post[i]),
            row(g_ffn_pre[i]), wg, wu, wd, row(g_ffn_post[i]), wpp, row(g_ple[i]), wpg,
        ).reshape(b, s, D_MODEL)
    return h
```

```python
import functools
import math

import jax
import jax.numpy as jnp
import numpy as np
from jax import lax
from jax.experimental import pallas as pl
from jax.experimental.pallas import tpu as pltpu

D_MODEL = 1024
D_PLE = 256
ATTN_HEADS = 8
HEAD_DIM = 64
D_ATTN = ATTN_HEADS * HEAD_DIM
POOL_WINDOWS = (2, 4, 8, 16)
POOL_GROUPS = len(POOL_WINDOWS)
POOL_CH = 128
D_POOL = POOL_GROUPS * POOL_CH
D_FF = 2816
RMS_EPS = 1e-6
LOG2E = math.log2(math.e)

LANES = 128
BF16_SUBLANES = 16
HALO = 16
CUMSUM_CHUNK = 256
N_PIECES = 3
PIECE_LANES = N_PIECES * ATTN_HEADS
U_COL = LANES
QKV_COL = U_COL + D_POOL
W1_COLS = QKV_COL + 3 * D_ATTN

PROJ_ROWS = 1024
ATTN_Q_ROWS = 512
ITEMS_PER_TRIP = 8
ITEMS_PER_SHORT_TRIP = 4
STATS_ROWS = 8
SKIP_LOG2 = 160.0
NORM_SLACK = 1.01
TAIL_ROWS = 1024
FF_CHUNK = 256
VMEM_LIMIT_BYTES = 59 * 1024 * 1024

BF16 = jnp.bfloat16
F32 = jnp.float32


def _aug_lane(head):
    return HEAD_DIM if head % 2 == 0 else 0


def _rms(x, g):
    ms = jnp.mean(x * x, axis=-1, keepdims=True)
    return x * lax.rsqrt(ms + RMS_EPS) * g


def _split3(x):
    hi = x.astype(BF16).astype(F32)
    r1 = x - hi
    mid = r1.astype(BF16).astype(F32)
    lo = (r1 - mid).astype(BF16).astype(F32)
    return hi, mid, lo


def _by_piece(lane, first, second, third):
    return jnp.where(lane < ATTN_HEADS, first, jnp.where(lane < 2 * ATTN_HEADS, second, third))


def _proj_kernel(x_ref, g_ref, w1_ref, bf_ref, hsum_ref, wpool_ref, pscale_ref, gpool_ref,
                 q_ref, k_ref, v_ref, nm_ref, stats_ref, carry_ref, halo_ref):
    tile = pl.program_id(1)
    tm = x_ref.shape[1]

    @pl.when(tile == 0)
    def _():
        carry_ref[...] = jnp.zeros_like(carry_ref)
        halo_ref[...] = jnp.zeros_like(halo_ref)

    hn = _rms(x_ref[0], g_ref[...]).astype(BF16)
    z = jnp.dot(hn, w1_ref[...], preferred_element_type=F32)
    lane = lax.broadcasted_iota(jnp.int32, (tm, LANES), 1)

    zf = z[:, :U_COL] + bf_ref[...]
    logf = jnp.minimum(zf, 0.0) - jnp.log1p(jnp.exp(-jnp.abs(zf)))
    logf = jnp.where(lane < PIECE_LANES, logf, 0.0)
    pieces = _by_piece(lane, *_split3(logf)).astype(BF16)
    r = lax.broadcasted_iota(jnp.int32, (CUMSUM_CHUNK, CUMSUM_CHUNK), 0)
    c = lax.broadcasted_iota(jnp.int32, (CUMSUM_CHUNK, CUMSUM_CHUNK), 1)
    tri = jnp.where(c <= r, 1.0, 0.0).astype(BF16)
    carry = carry_ref[0:1, :]
    chunks = []
    for r0 in range(0, tm, CUMSUM_CHUNK):
        cs = carry + jnp.dot(tri, pieces[r0:r0 + CUMSUM_CHUNK], preferred_element_type=F32)
        carry = cs[CUMSUM_CHUNK - 1:CUMSUM_CHUNK, :]
        chunks.append(cs)
    carry_ref[0:1, :] = carry
    part = jnp.concatenate(chunks, axis=0)
    total = (part + pltpu.roll(part, LANES - ATTN_HEADS, axis=1)
             + pltpu.roll(part, LANES - 2 * ATTN_HEADS, axis=1))
    negc = -LOG2E * total
    negc3 = _by_piece(lane, negc, pltpu.roll(negc, ATTN_HEADS, axis=1),
                      pltpu.roll(negc, 2 * ATTN_HEADS, axis=1))
    cp_odd = jnp.where(lane < PIECE_LANES, _by_piece(lane, *_split3(negc3)), 0.0)
    cp_even = pltpu.roll(cp_odd, HEAD_DIM, axis=1)

    for h in range(ATTN_HEADS):
        first = QKV_COL + (h // 2) * LANES
        slab = lambda section: z[:, section * D_ATTN + first:section * D_ATTN + first + LANES]
        a0 = _aug_lane(h)
        data = (lane < HEAD_DIM) if h % 2 == 0 else (lane >= HEAD_DIM)
        own = ((lane == a0 + h) | (lane == a0 + ATTN_HEADS + h)
               | (lane == a0 + 2 * ATTN_HEADS + h))
        q_ref[0, h] = jnp.where(data, LOG2E * slab(0), jnp.where(own, 1.0, 0.0)).astype(BF16)
        k_ref[0, h] = jnp.where(data, slab(1), cp_even if h % 2 == 0 else cp_odd).astype(BF16)
        v_ref[0, h] = jnp.where(data, slab(2), jnp.where(lane == a0, 1.0, 0.0)).astype(BF16)

    zqk = z[:, QKV_COL:QKV_COL + 2 * D_ATTN]
    sqn = jnp.dot((zqk * zqk).astype(BF16), hsum_ref[...], preferred_element_type=F32)
    for t in range(tm // ATTN_Q_ROWS):
        lo, hi = t * ATTN_Q_ROWS, (t + 1) * ATTN_Q_ROWS
        stats_ref[0, t, 0:1, :] = negc[lo:lo + 1, :]
        stats_ref[0, t, 1:2, :] = negc[hi - 1:hi, :]
        stats_ref[0, t, 2:3, :] = jnp.max(sqn[lo:hi], axis=0, keepdims=True)
        stats_ref[0, t, 3:8, :] = jnp.zeros((5, LANES), F32)

    u = z[:, U_COL:QKV_COL]
    ue = jnp.concatenate([halo_ref[...], u], axis=0)
    halo_ref[...] = u[tm - HALO:, :]
    seen = (tile * tm + 1 + lax.broadcasted_iota(jnp.int32, (tm, POOL_CH), 0)).astype(F32)
    ys = []
    for g, w in enumerate(POOL_WINDOWS):
        ch = slice(g * POOL_CH, (g + 1) * POOL_CH)
        s = ue[:, ch]
        shift = 1
        while shift < w:
            s = s + pltpu.roll(s, shift, axis=0)
            shift *= 2
        inv_count = 1.0 / jnp.minimum(seen, float(w))
        ys.append((s[HALO:] * inv_count - u[:, ch]).astype(BF16))
    ms = []
    for pair in range(POOL_GROUPS // 2):
        cols = slice(2 * pair * POOL_CH, (2 * pair + 2) * POOL_CH)
        m2 = jnp.dot(jnp.concatenate(ys[2 * pair:2 * pair + 2], axis=1), wpool_ref[pair],
                     preferred_element_type=F32) * pscale_ref[:, cols]
        ms += [m2[:, :POOL_CH], m2[:, POOL_CH:]]
    msq = sum(jnp.sum(m * m, axis=-1, keepdims=True) for m in ms) * (1.0 / D_POOL)
    rstd = lax.rsqrt(msq + RMS_EPS)
    for g in range(POOL_GROUPS):
        ch = slice(g * POOL_CH, (g + 1) * POOL_CH)
        nm_ref[0, :, ch] = (ms[g] * rstd * gpool_ref[:, ch]).astype(BF16)


def _proj_call(x, g_pre, w1, bf3, hsum, wpool2, pscale, gpool):
    b, s, _ = x.shape
    tm = PROJ_ROWS
    const = lambda *shape: pl.BlockSpec(shape, lambda bi, i: (0,) * len(shape))
    head_spec = pl.BlockSpec((1, ATTN_HEADS, tm, LANES), lambda bi, i: (bi, 0, i, 0))
    head_shape = jax.ShapeDtypeStruct((b, ATTN_HEADS, s, LANES), BF16)
    return pl.pallas_call(
        _proj_kernel,
        grid=(b, s // tm),
        in_specs=[
            pl.BlockSpec((1, tm, D_MODEL), lambda bi, i: (bi, i, 0)),
            const(1, D_MODEL),
            const(D_MODEL, W1_COLS),
            const(1, LANES),
            const(2 * D_ATTN, LANES),
            const(POOL_GROUPS // 2, 2 * POOL_CH, 2 * POOL_CH),
            const(1, D_POOL),
            const(1, D_POOL),
        ],
        out_specs=[head_spec, head_spec, head_spec,
                   pl.BlockSpec((1, tm, D_POOL), lambda bi, i: (bi, i, 0)),
                   pl.BlockSpec((1, tm // ATTN_Q_ROWS, STATS_ROWS, LANES),
                                lambda bi, i: (bi, i, 0, 0))],
        out_shape=[head_shape, head_shape, head_shape,
                   jax.ShapeDtypeStruct((b, s, D_POOL), BF16),
                   jax.ShapeDtypeStruct((b, s // ATTN_Q_ROWS, STATS_ROWS, LANES), F32)],
        scratch_shapes=[pltpu.VMEM((8, LANES), F32), pltpu.VMEM((HALO, D_POOL), F32)],
        compiler_params=pltpu.CompilerParams(
            dimension_semantics=("arbitrary", "arbitrary"),
            vmem_limit_bytes=VMEM_LIMIT_BYTES),
        name="proj",
    )(x, g_pre, w1, bf3, hsum, wpool2, pscale, gpool)


def _attn_kernel(n_cast, items_ref, ntrips_ref, q_ref, k_ref, v_ref, *refs):
    cast_in, o_ref, cast_out = refs[:n_cast], refs[n_cast], refs[n_cast + 1:2 * n_cast + 1]
    m_ref, acc_ref = refs[2 * n_cast + 1:]
    step_id = pl.program_id(0) * pl.num_programs(1) + pl.program_id(1)
    i = pl.program_id(1)
    nh = q_ref.shape[1]
    tq = q_ref.shape[2]
    n_codes = pl.num_programs(1) * nh
    nt = (((1,), (1,)), ((), ()))
    lane = lax.broadcasted_iota(jnp.int32, (tq, LANES), 1)

    for src, dst in zip(cast_in, cast_out):
        dst[...] = src[...].astype(BF16)

    @pl.when(step_id == 0)
    def _():
        m_ref[...] = jnp.full(m_ref.shape, -jnp.inf, F32)
        acc_ref[...] = jnp.zeros(acc_ref.shape, F32)

    def qk(item):
        head, _, q_rows, k_rows, _ = item
        return lax.dot_general(q_ref[0, head, q_rows, :], k_ref[0, head, k_rows, :], nt,
                               preferred_element_type=F32)

    def run(items, ahead, last):
        scores = [qk(item) for item in items[:ahead]]
        for n, (head, slot, q_rows, k_rows, mask) in enumerate(items):
            s = scores.pop(0)
            if n + ahead < len(items):
                scores.append(qk(items[n + ahead]))
            if mask is not None:
                s = jnp.where(mask, s, -jnp.inf)
            m = m_ref[slot, q_rows, :]
            m_new = jnp.maximum(m, jnp.max(s, axis=-1, keepdims=True))
            p = jnp.exp2(s - jnp.concatenate([m_new] * (s.shape[1] // LANES), axis=1))
            alpha = jnp.exp2(m - m_new)
            acc_ref[slot, q_rows, :] = alpha * acc_ref[slot, q_rows, :] + jnp.dot(
                p.astype(BF16), v_ref[0, head, k_rows, :], preferred_element_type=F32)
            m_ref[slot, q_rows, :] = jnp.full_like(m_new, -jnp.inf) if last else m_new

    all_rows = slice(0, tq)
    first_item = step_id * n_codes

    def scheduled(start, count):
        items = []
        for n in range(count):
            code = items_ref[start + n]
            valid = code < n_codes
            head = jnp.where(valid, code & (nh - 1), 0)
            block = jnp.where(valid, lax.shift_right_logical(code, nh.bit_length() - 1), 0)
            items.append((head, jnp.where(valid, head, nh), all_rows,
                          pl.ds(pl.multiple_of(block * tq, tq), tq), None))
        run(items, ahead=2, last=False)

    n_long = ntrips_ref[2 * step_id]

    @pl.loop(0, n_long)
    def _(trip):
        scheduled(first_item + trip * ITEMS_PER_TRIP, ITEMS_PER_TRIP)

    @pl.loop(0, ntrips_ref[2 * step_id + 1])
    def _(trip):
        scheduled(first_item + n_long * ITEMS_PER_TRIP, ITEMS_PER_SHORT_TRIP)

    r = lax.broadcasted_iota(jnp.int32, (tq, tq), 0)
    c = lax.broadcasted_iota(jnp.int32, (tq, tq), 1)
    own_keys = pl.ds(pl.multiple_of(i * tq, tq), tq)
    run([(hh, hh, all_rows, own_keys, c <= r) for hh in range(nh)], ahead=2, last=True)
    for pair in range(nh // 2):
        outs = []
        for hh in (2 * pair, 2 * pair + 1):
            acc = acc_ref[hh]
            l_lane = _aug_lane(hh)
            outs.append(acc / acc[:, l_lane:l_lane + 1])
        o_ref[0, :, pair * LANES:(pair + 1) * LANES] = jnp.where(lane < HEAD_DIM, outs[0], outs[1])


def _attention_schedule(stats):
    nb = stats.shape[1]
    nh = ATTN_HEADS
    neg_first = stats[:, :, 0, :nh]
    neg_last = stats[:, :, 1, :nh]
    qn = jnp.sqrt(stats[:, :, 2, :nh]) * (LOG2E * NORM_SLACK)
    kn = jnp.sqrt(stats[:, :, 2, nh:2 * nh]) * NORM_SLACK
    top = neg_last[:, None, :, :] + qn[:, :, None, :] * kn[:, None, :, :]
    floor = neg_first[:, :, None, :] - (qn * kn)[:, :, None, :]
    tile_i = lax.broadcasted_iota(jnp.int32, (1, nb, nb, nh), 1)
    block_j = lax.broadcasted_iota(jnp.int32, (1, nb, nb, nh), 2)
    head = lax.broadcasted_iota(jnp.int32, (1, nb, nb, nh), 3)
    need = (block_j < tile_i) & ~(top - floor <= -SKIP_LOG2)
    n_codes = nb * nh
    codes = jnp.where(need, block_j * nh + head, n_codes).reshape(stats.shape[0], nb, n_codes)
    items = jnp.sort(codes, axis=-1).astype(jnp.int32)
    count = jnp.sum(need, axis=(2, 3)).astype(jnp.int32)
    rest = count % ITEMS_PER_TRIP
    n_long = count // ITEMS_PER_TRIP + (rest > ITEMS_PER_SHORT_TRIP)
    n_short = (rest > 0) & (rest <= ITEMS_PER_SHORT_TRIP)
    ntrips = jnp.stack([n_long, n_short.astype(jnp.int32)], axis=-1)
    return items.reshape(-1), ntrips.reshape(-1)


def _cast_block_rows(rows, steps):
    for block in range(BF16_SUBLANES, rows + 1, BF16_SUBLANES):
        if rows % block == 0 and rows // block <= steps:
            return block
    raise ValueError(f"no bf16 row block for {rows} rows in {steps} steps")


def _attn_call(q, k, v, stats, cast_weights):
    b, nh, s, _ = q.shape
    tq = ATTN_Q_ROWS
    nq = s // tq
    items, ntrips = _attention_schedule(stats)
    kv_spec = pl.BlockSpec((1, nh, s, LANES), lambda bi, i, *_: (bi, 0, 0, 0))
    cast_specs = []
    for w in cast_weights:
        rows, cols = w.shape
        block = _cast_block_rows(rows, b * nq)
        last = rows // block - 1
        cast_specs.append(pl.BlockSpec(
            (block, cols), lambda bi, i, *_, last=last: (jnp.minimum(bi * nq + i, last), 0)))
    outs = pl.pallas_call(
        functools.partial(_attn_kernel, len(cast_weights)),
        grid_spec=pltpu.PrefetchScalarGridSpec(
            num_scalar_prefetch=2,
            grid=(b, nq),
            in_specs=[pl.BlockSpec((1, nh, tq, LANES), lambda bi, i, *_: (bi, 0, i, 0)),
                      kv_spec, kv_spec] + cast_specs,
            out_specs=[pl.BlockSpec((1, tq, nh * HEAD_DIM), lambda bi, i, *_: (bi, i, 0))]
            + cast_specs,
            scratch_shapes=[pltpu.VMEM((nh + 1, tq, LANES), F32)] * 2,
        ),
        out_shape=[jax.ShapeDtypeStruct((b, s, D_ATTN), F32)]
        + [jax.ShapeDtypeStruct(w.shape, BF16) for w in cast_weights],
        compiler_params=pltpu.CompilerParams(
            dimension_semantics=("arbitrary", "arbitrary"),
            vmem_limit_bytes=VMEM_LIMIT_BYTES),
        name="fox_attn",
    )(items, ntrips, q, k, v, *cast_weights)
    return outs[0], outs[1:]


def _tail_kernel(x_ref, a_ref, nm_ref, p_ref, gattn_ref, wo_ref, gpost_ref, gffn_ref,
                 wg_ref, wu_ref, wd_ref, gffpost_ref, wpp_ref, gple_ref, wpg_ref, o_ref):
    tm = x_ref.shape[0]
    halves = (slice(0, tm // 2), slice(tm // 2, tm))

    def swiglu(hn, cols):
        gate = jnp.dot(hn, wg_ref[:, cols], preferred_element_type=F32)
        up = jnp.dot(hn, wu_ref[:, cols], preferred_element_type=F32)
        act = (gate * jax.nn.sigmoid(gate) * up).astype(BF16)
        return jnp.dot(act, wd_ref[cols, :], preferred_element_type=F32)

    na = _rms(a_ref[...], gattn_ref[...]).astype(BF16)
    mixes = [jnp.dot(na[rows], wo_ref[:D_ATTN, :], preferred_element_type=F32)
             + jnp.dot(nm_ref[rows, :], wo_ref[D_ATTN:, :], preferred_element_type=F32)
             for rows in halves]
    hs, hns, ffs = [], [], []
    for rows, mix in zip(halves, mixes):
        h = x_ref[rows, :] + _rms(mix, gpost_ref[...])
        hn = _rms(h, gffn_ref[...]).astype(BF16)
        hs.append(h)
        hns.append(hn)
        ffs.append(swiglu(hn, slice(0, FF_CHUNK)))
    hn = jnp.concatenate(hns, axis=0)
    ff = jnp.concatenate(ffs, axis=0)
    for c0 in range(FF_CHUNK, D_FF, FF_CHUNK):
        ff = ff + swiglu(hn, slice(c0, c0 + FF_CHUNK))

    e = _rms(jnp.dot(p_ref[...].astype(BF16), wpp_ref[...], preferred_element_type=F32),
             gple_ref[...])
    for rows, h in zip(halves, hs):
        h = h + _rms(ff[rows], gffpost_ref[...])
        gate = jax.nn.sigmoid(jnp.dot(h.astype(BF16), wpg_ref[...], preferred_element_type=F32))
        o_ref[rows, :] = h + gate * e[rows]


def _tail_call(x2, a2, nm2, p2, gattn, wo, gpost, gffn, wg, wu, wd, gffpost, wpp, gple, wpg):
    t = x2.shape[0]
    tm = TAIL_ROWS
    rows = lambda width: pl.BlockSpec((tm, width), lambda i: (i, 0))
    const = lambda *shape: pl.BlockSpec(shape, lambda i: (0,) * len(shape),
                                        pipeline_mode=pl.Buffered(1))
    return pl.pallas_call(
        _tail_kernel,
        grid=(t // tm,),
        in_specs=[
            rows(D_MODEL), rows(D_ATTN), rows(D_POOL), rows(D_PLE),
            const(1, D_ATTN), const(D_MODEL, D_MODEL), const(1, D_MODEL), const(1, D_MODEL),
            const(D_MODEL, D_FF), const(D_MODEL, D_FF), const(D_FF, D_MODEL), const(1, D_MODEL),
            const(D_PLE, D_MODEL), const(1, D_MODEL), const(D_MODEL, D_MODEL),
        ],
        out_specs=rows(D_MODEL),
        out_shape=jax.ShapeDtypeStruct((t, D_MODEL), F32),
        compiler_params=pltpu.CompilerParams(
            dimension_semantics=("arbitrary",),
            vmem_limit_bytes=VMEM_LIMIT_BYTES),
        name="tail",
    )(x2, a2, nm2, p2, gattn, wo, gpost, gffn, wg, wu, wd, gffpost, wpp, gple, wpg)


def _head_sum_matrix():
    hsum = np.zeros((2 * D_ATTN, LANES), np.float32)
    for d in range(2 * D_ATTN):
        hsum[d, d // HEAD_DIM] = 1.0
    return jnp.asarray(hsum, BF16)


def kernel(x, p, g_mix_pre, w_in, b_forget, g_attn_grp, g_pool_grp, w_pool, pool_scale, w_out,
           g_mix_post, g_ffn_pre, w_ffn_gate, w_ffn_up, w_ffn_down, g_ffn_post, w_ple_proj,
           g_ple, w_ple_gate):
    b, s, _ = x.shape
    depth = w_in.shape[0]
    assert s % PROJ_ROWS == 0 and PROJ_ROWS % ATTN_Q_ROWS == 0 and (b * s) % TAIL_ROWS == 0
    assert (s // ATTN_Q_ROWS * ATTN_HEADS) % ITEMS_PER_TRIP == 0
    hsum = _head_sum_matrix()
    row = lambda v: v.reshape(1, -1).astype(F32)
    h = x
    for i in range(depth):
        w = w_in[i]
        o_f = 3 * D_ATTN
        o_u = o_f + ATTN_HEADS
        wf = w[:, o_f:o_u]
        w1 = jnp.concatenate(
            [wf, wf, wf, jnp.zeros((D_MODEL, LANES - PIECE_LANES), F32), w[:, o_u:],
             w[:, :D_ATTN] * (1.0 / math.sqrt(HEAD_DIM)), w[:, D_ATTN:o_f]],
            axis=1).astype(BF16)
        bf = b_forget[i].astype(F32)
        bf3 = jnp.concatenate([bf, bf, bf, jnp.zeros((LANES - PIECE_LANES,), F32)])
        wp = w_pool[i].astype(BF16)
        zero = jnp.zeros((POOL_CH, POOL_CH), BF16)
        wpool2 = jnp.stack([jnp.block([[wp[2 * j], zero], [zero, wp[2 * j + 1]]])
                            for j in range(POOL_GROUPS // 2)])
        q, k, v, nm, stats = _proj_call(
            h, row(g_mix_pre[i]), w1, bf3.reshape(1, LANES), hsum,
            wpool2, row(pool_scale[i]), row(g_pool_grp[i]))
        a, (wo, wg, wu, wd, wpp, wpg) = _attn_call(
            q, k, v, stats, [w_out[i], w_ffn_gate[i], w_ffn_up[i], w_ffn_down[i],
                             w_ple_proj[i], w_ple_gate[i]])
        t = b * s
        h = _tail_call(
            h.reshape(t, D_MODEL), a.reshape(t, D_ATTN), nm.reshape(t, D_POOL),
            p[i].reshape(t, D_PLE), row(g_attn_grp[i]), wo, row(g_mix_post[i]),
            row(g_ffn_pre[i]), wg, wu, wd, row(g_ffn_post[i]), wpp, row(g_ple[i]), wpg,
        ).reshape(b, s, D_MODEL)
    return h
```

```python
import functools
import math

import jax
import jax.numpy as jnp
import numpy as np
from jax import lax
from jax.experimental import pallas as pl
from jax.experimental.pallas import tpu as pltpu

D_MODEL = 1024
D_PLE = 256
ATTN_HEADS = 8
HEAD_DIM = 64
D_ATTN = ATTN_HEADS * HEAD_DIM
POOL_WINDOWS = (2, 4, 8, 16)
POOL_GROUPS = len(POOL_WINDOWS)
POOL_CH = 128
D_POOL = POOL_GROUPS * POOL_CH
D_FF = 2816
RMS_EPS = 1e-6
LOG2E = math.log2(math.e)

LANES = 128
BF16_SUBLANES = 16
HALO = 16
CUMSUM_CHUNK = 256
WEIGHT_CAST_ROWS = 128
N_PIECES = 3
PIECE_LANES = N_PIECES * ATTN_HEADS
U_COL = LANES
QKV_COL = U_COL + D_POOL
W1_COLS = QKV_COL + 3 * D_ATTN

PROJ_ROWS = 1024
ATTN_Q_ROWS = 512
ITEMS_PER_TRIP = 8
ITEMS_PER_SHORT_TRIP = 4
STATS_ROWS = 8
SKIP_LOG2 = 160.0
NORM_SLACK = 1.01
TAIL_ROWS = 1024
FF_CHUNK = 256
VMEM_LIMIT_BYTES = 59 * 1024 * 1024

BF16 = jnp.bfloat16
F32 = jnp.float32


def _aug_lane(head):
    return HEAD_DIM if head % 2 == 0 else 0


def _rms(x, g):
    ms = jnp.mean(x * x, axis=-1, keepdims=True)
    return x * lax.rsqrt(ms + RMS_EPS) * g


def _split3(x):
    hi = x.astype(BF16).astype(F32)
    r1 = x - hi
    mid = r1.astype(BF16).astype(F32)
    lo = (r1 - mid).astype(BF16).astype(F32)
    return hi, mid, lo


def _by_piece(lane, first, second, third):
    return jnp.where(lane < ATTN_HEADS, first, jnp.where(lane < 2 * ATTN_HEADS, second, third))


def _proj_kernel(x_ref, g_ref, wfu_ref, wqkv_ref, bf_ref, hsum_ref, wpool_ref, pscale_ref,
                 gpool_ref, q_ref, k_ref, v_ref, nm_ref, stats_ref, wqkv_bf, carry_ref, halo_ref):
    tile = pl.program_id(1)
    tm = x_ref.shape[1]

    @pl.when((pl.program_id(0) == 0) & (tile == 0))
    def _():
        for r0 in range(0, 3 * D_ATTN, WEIGHT_CAST_ROWS):
            scale = 1.0 / math.sqrt(HEAD_DIM) if r0 < D_ATTN else 1.0
            rows = slice(r0, r0 + WEIGHT_CAST_ROWS)
            wqkv_bf[rows, :] = (wqkv_ref[rows, :] * scale).astype(BF16)

    @pl.when(tile == 0)
    def _():
        carry_ref[...] = jnp.zeros_like(carry_ref)
        halo_ref[...] = jnp.zeros_like(halo_ref)

    hn = _rms(x_ref[0], g_ref[...]).astype(BF16)
    nt = (((1,), (1,)), ((), ()))
    z = jnp.concatenate(
        [lax.dot_general(hn, wfu_ref[...], nt, preferred_element_type=F32),
         lax.dot_general(hn, wqkv_bf[...], nt, preferred_element_type=F32)],
        axis=1)
    lane = lax.broadcasted_iota(jnp.int32, (tm, LANES), 1)

    zf = z[:, :U_COL] + bf_ref[...]
    logf = jnp.minimum(zf, 0.0) - jnp.log1p(jnp.exp(-jnp.abs(zf)))
    logf = jnp.where(lane < PIECE_LANES, logf, 0.0)
    pieces = _by_piece(lane, *_split3(logf)).astype(BF16)
    r = lax.broadcasted_iota(jnp.int32, (CUMSUM_CHUNK, CUMSUM_CHUNK), 0)
    c = lax.broadcasted_iota(jnp.int32, (CUMSUM_CHUNK, CUMSUM_CHUNK), 1)
    tri = jnp.where(c <= r, 1.0, 0.0).astype(BF16)
    carry = carry_ref[0:1, :]
    chunks = []
    for r0 in range(0, tm, CUMSUM_CHUNK):
        cs = carry + jnp.dot(tri, pieces[r0:r0 + CUMSUM_CHUNK], preferred_element_type=F32)
        carry = cs[CUMSUM_CHUNK - 1:CUMSUM_CHUNK, :]
        chunks.append(cs)
    carry_ref[0:1, :] = carry
    part = jnp.concatenate(chunks, axis=0)
    total = (part + pltpu.roll(part, LANES - ATTN_HEADS, axis=1)
             + pltpu.roll(part, LANES - 2 * ATTN_HEADS, axis=1))
    negc = -LOG2E * total
    negc3 = _by_piece(lane, negc, pltpu.roll(negc, ATTN_HEADS, axis=1),
                      pltpu.roll(negc, 2 * ATTN_HEADS, axis=1))
    cp_odd = jnp.where(lane < PIECE_LANES, _by_piece(lane, *_split3(negc3)), 0.0)
    cp_even = pltpu.roll(cp_odd, HEAD_DIM, axis=1)

    for h in range(ATTN_HEADS):
        first = QKV_COL + (h // 2) * LANES
        slab = lambda section: z[:, section * D_ATTN + first:section * D_ATTN + first + LANES]
        a0 = _aug_lane(h)
        data = (lane < HEAD_DIM) if h % 2 == 0 else (lane >= HEAD_DIM)
        own = ((lane == a0 + h) | (lane == a0 + ATTN_HEADS + h)
               | (lane == a0 + 2 * ATTN_HEADS + h))
        q_ref[0, h] = jnp.where(data, LOG2E * slab(0), jnp.where(own, 1.0, 0.0)).astype(BF16)
        k_ref[0, h] = jnp.where(data, slab(1), cp_even if h % 2 == 0 else cp_odd).astype(BF16)
        v_ref[0, h] = jnp.where(data, slab(2), jnp.where(lane == a0, 1.0, 0.0)).astype(BF16)

    zqk = z[:, QKV_COL:QKV_COL + 2 * D_ATTN]
    sqn = jnp.dot((zqk * zqk).astype(BF16), hsum_ref[...], preferred_element_type=F32)
    for t in range(tm // ATTN_Q_ROWS):
        lo, hi = t * ATTN_Q_ROWS, (t + 1) * ATTN_Q_ROWS
        stats_ref[0, t, 0:1, :] = negc[lo:lo + 1, :]
        stats_ref[0, t, 1:2, :] = negc[hi - 1:hi, :]
        stats_ref[0, t, 2:3, :] = jnp.max(sqn[lo:hi], axis=0, keepdims=True)
        stats_ref[0, t, 3:8, :] = jnp.zeros((5, LANES), F32)

    u = z[:, U_COL:QKV_COL]
    ue = jnp.concatenate([halo_ref[...], u], axis=0)
    halo_ref[...] = u[tm - HALO:, :]
    seen = (tile * tm + 1 + lax.broadcasted_iota(jnp.int32, (tm, POOL_CH), 0)).astype(F32)
    ys = []
    for g, w in enumerate(POOL_WINDOWS):
        ch = slice(g * POOL_CH, (g + 1) * POOL_CH)
        s = ue[:, ch]
        shift = 1
        while shift < w:
            s = s + pltpu.roll(s, shift, axis=0)
            shift *= 2
        inv_count = 1.0 / jnp.minimum(seen, float(w))
        ys.append((s[HALO:] * inv_count - u[:, ch]).astype(BF16))
    ms = []
    for pair in range(POOL_GROUPS // 2):
        cols = slice(2 * pair * POOL_CH, (2 * pair + 2) * POOL_CH)
        m2 = jnp.dot(jnp.concatenate(ys[2 * pair:2 * pair + 2], axis=1), wpool_ref[pair],
                     preferred_element_type=F32) * pscale_ref[:, cols]
        ms += [m2[:, :POOL_CH], m2[:, POOL_CH:]]
    msq = sum(jnp.sum(m * m, axis=-1, keepdims=True) for m in ms) * (1.0 / D_POOL)
    rstd = lax.rsqrt(msq + RMS_EPS)
    for g in range(POOL_GROUPS):
        ch = slice(g * POOL_CH, (g + 1) * POOL_CH)
        nm_ref[0, :, ch] = (ms[g] * rstd * gpool_ref[:, ch]).astype(BF16)


def _proj_call(x, g_pre, wfu_t, w_in_t, bf3, hsum, wpool2, pscale, gpool):
    b, s, _ = x.shape
    tm = PROJ_ROWS
    const = lambda *shape: pl.BlockSpec(shape, lambda bi, i: (0,) * len(shape))
    head_spec = pl.BlockSpec((1, ATTN_HEADS, tm, LANES), lambda bi, i: (bi, 0, i, 0))
    head_shape = jax.ShapeDtypeStruct((b, ATTN_HEADS, s, LANES), BF16)
    return pl.pallas_call(
        _proj_kernel,
        grid=(b, s // tm),
        in_specs=[
            pl.BlockSpec((1, tm, D_MODEL), lambda bi, i: (bi, i, 0)),
            const(1, D_MODEL),
            const(QKV_COL, D_MODEL),
            pl.BlockSpec((3 * D_ATTN, D_MODEL), lambda bi, i: (0, 0),
                         pipeline_mode=pl.Buffered(1)),
            const(1, LANES),
            const(2 * D_ATTN, LANES),
            const(POOL_GROUPS // 2, 2 * POOL_CH, 2 * POOL_CH),
            const(1, D_POOL),
            const(1, D_POOL),
        ],
        out_specs=[head_spec, head_spec, head_spec,
                   pl.BlockSpec((1, tm, D_POOL), lambda bi, i: (bi, i, 0)),
                   pl.BlockSpec((1, tm // ATTN_Q_ROWS, STATS_ROWS, LANES),
                                lambda bi, i: (bi, i, 0, 0))],
        out_shape=[head_shape, head_shape, head_shape,
                   jax.ShapeDtypeStruct((b, s, D_POOL), BF16),
                   jax.ShapeDtypeStruct((b, s // ATTN_Q_ROWS, STATS_ROWS, LANES), F32)],
        scratch_shapes=[pltpu.VMEM((3 * D_ATTN, D_MODEL), BF16),
                        pltpu.VMEM((8, LANES), F32), pltpu.VMEM((HALO, D_POOL), F32)],
        compiler_params=pltpu.CompilerParams(
            dimension_semantics=("arbitrary", "arbitrary"),
            vmem_limit_bytes=VMEM_LIMIT_BYTES),
        name="proj",
    )(x, g_pre, wfu_t, w_in_t, bf3, hsum, wpool2, pscale, gpool)


def _attn_kernel(n_cast, items_ref, ntrips_ref, q_ref, k_ref, v_ref, *refs):
    cast_in, o_ref, cast_out = refs[:n_cast], refs[n_cast], refs[n_cast + 1:2 * n_cast + 1]
    m_ref, acc_ref = refs[2 * n_cast + 1:]
    step_id = pl.program_id(0) * pl.num_programs(1) + pl.program_id(1)
    i = pl.program_id(1)
    nh = q_ref.shape[1]
    tq = q_ref.shape[2]
    n_codes = pl.num_programs(1) * nh
    nt = (((1,), (1,)), ((), ()))
    lane = lax.broadcasted_iota(jnp.int32, (tq, LANES), 1)

    @pl.when(step_id == 0)
    def _():
        m_ref[...] = jnp.full(m_ref.shape, -jnp.inf, F32)
        acc_ref[...] = jnp.zeros(acc_ref.shape, F32)

    def qk(item):
        head, _, q_rows, k_rows, _ = item
        return lax.dot_general(q_ref[0, head, q_rows, :], k_ref[0, head, k_rows, :], nt,
                               preferred_element_type=F32)

    def run(items, ahead, last):
        scores = [qk(item) for item in items[:ahead]]
        for n, (head, slot, q_rows, k_rows, mask) in enumerate(items):
            s = scores.pop(0)
            if n + ahead < len(items):
                scores.append(qk(items[n + ahead]))
            if mask is not None:
                s = jnp.where(mask, s, -jnp.inf)
            m = m_ref[slot, q_rows, :]
            m_new = jnp.maximum(m, jnp.max(s, axis=-1, keepdims=True))
            p = jnp.exp2(s - jnp.concatenate([m_new] * (s.shape[1] // LANES), axis=1))
            alpha = jnp.exp2(m - m_new)
            acc_ref[slot, q_rows, :] = alpha * acc_ref[slot, q_rows, :] + jnp.dot(
                p.astype(BF16), v_ref[0, head, k_rows, :], preferred_element_type=F32)
            m_ref[slot, q_rows, :] = jnp.full_like(m_new, -jnp.inf) if last else m_new

    all_rows = slice(0, tq)
    first_item = step_id * n_codes

    def scheduled(start, count):
        items = []
        for n in range(count):
            code = items_ref[start + n]
            valid = code < n_codes
            head = jnp.where(valid, code & (nh - 1), 0)
            block = jnp.where(valid, lax.shift_right_logical(code, nh.bit_length() - 1), 0)
            items.append((head, jnp.where(valid, head, nh), all_rows,
                          pl.ds(pl.multiple_of(block * tq, tq), tq), None))
        run(items, ahead=2, last=False)

    n_long = ntrips_ref[2 * step_id]

    @pl.loop(0, n_long)
    def _(trip):
        scheduled(first_item + trip * ITEMS_PER_TRIP, ITEMS_PER_TRIP)

    @pl.loop(0, ntrips_ref[2 * step_id + 1])
    def _(trip):
        scheduled(first_item + n_long * ITEMS_PER_TRIP, ITEMS_PER_SHORT_TRIP)

    for src, dst in zip(cast_in, cast_out):
        dst[...] = src[...].astype(BF16)

    r = lax.broadcasted_iota(jnp.int32, (tq, tq), 0)
    c = lax.broadcasted_iota(jnp.int32, (tq, tq), 1)
    own_keys = pl.ds(pl.multiple_of(i * tq, tq), tq)
    run([(hh, hh, all_rows, own_keys, c <= r) for hh in range(nh)], ahead=2, last=True)
    for pair in range(nh // 2):
        outs = []
        for hh in (2 * pair, 2 * pair + 1):
            acc = acc_ref[hh]
            l_lane = _aug_lane(hh)
            outs.append(acc / acc[:, l_lane:l_lane + 1])
        o_ref[0, :, pair * LANES:(pair + 1) * LANES] = jnp.where(lane < HEAD_DIM, outs[0], outs[1])


def _attention_schedule(stats):
    nb = stats.shape[1]
    nh = ATTN_HEADS
    neg_first = stats[:, :, 0, :nh]
    neg_last = stats[:, :, 1, :nh]
    qn = jnp.sqrt(stats[:, :, 2, :nh]) * (LOG2E * NORM_SLACK)
    kn = jnp.sqrt(stats[:, :, 2, nh:2 * nh]) * NORM_SLACK
    top = neg_last[:, None, :, :] + qn[:, :, None, :] * kn[:, None, :, :]
    floor = neg_first[:, :, None, :] - (qn * kn)[:, :, None, :]
    tile_i = lax.broadcasted_iota(jnp.int32, (1, nb, nb, nh), 1)
    block_j = lax.broadcasted_iota(jnp.int32, (1, nb, nb, nh), 2)
    head = lax.broadcasted_iota(jnp.int32, (1, nb, nb, nh), 3)
    need = (block_j < tile_i) & ~(top - floor <= -SKIP_LOG2)
    n_codes = nb * nh
    codes = jnp.where(need, block_j * nh + head, n_codes).reshape(stats.shape[0], nb, n_codes)
    items = jnp.sort(codes, axis=-1).astype(jnp.int32)
    count = jnp.sum(need, axis=(2, 3)).astype(jnp.int32)
    rest = count % ITEMS_PER_TRIP
    n_long = count // ITEMS_PER_TRIP + (rest > ITEMS_PER_SHORT_TRIP)
    n_short = (rest > 0) & (rest <= ITEMS_PER_SHORT_TRIP)
    ntrips = jnp.stack([n_long, n_short.astype(jnp.int32)], axis=-1)
    return items.reshape(-1), ntrips.reshape(-1)


def _cast_block_rows(rows, steps):
    for block in range(BF16_SUBLANES, rows + 1, BF16_SUBLANES):
        if rows % block == 0 and rows // block <= steps:
            return block
    raise ValueError(f"no bf16 row block for {rows} rows in {steps} steps")


def _attn_call(q, k, v, stats, cast_weights):
    b, nh, s, _ = q.shape
    tq = ATTN_Q_ROWS
    nq = s // tq
    items, ntrips = _attention_schedule(stats)
    kv_spec = pl.BlockSpec((1, nh, s, LANES), lambda bi, i, *_: (bi, 0, 0, 0))
    cast_specs = []
    for w in cast_weights:
        rows, cols = w.shape
        block = _cast_block_rows(rows, b * nq)
        last = rows // block - 1
        cast_specs.append(pl.BlockSpec(
            (block, cols), lambda bi, i, *_, last=last: (jnp.minimum(bi * nq + i, last), 0)))
    outs = pl.pallas_call(
        functools.partial(_attn_kernel, len(cast_weights)),
        grid_spec=pltpu.PrefetchScalarGridSpec(
            num_scalar_prefetch=2,
            grid=(b, nq),
            in_specs=[pl.BlockSpec((1, nh, tq, LANES), lambda bi, i, *_: (bi, 0, i, 0)),
                      kv_spec, kv_spec] + cast_specs,
            out_specs=[pl.BlockSpec((1, tq, nh * HEAD_DIM), lambda bi, i, *_: (bi, i, 0))]
            + cast_specs,
            scratch_shapes=[pltpu.VMEM((nh + 1, tq, LANES), F32)] * 2,
        ),
        out_shape=[jax.ShapeDtypeStruct((b, s, D_ATTN), F32)]
        + [jax.ShapeDtypeStruct(w.shape, BF16) for w in cast_weights],
        compiler_params=pltpu.CompilerParams(
            dimension_semantics=("arbitrary", "arbitrary"),
            vmem_limit_bytes=VMEM_LIMIT_BYTES),
        name="fox_attn",
    )(items, ntrips, q, k, v, *cast_weights)
    return outs[0], outs[1:]


def _tail_kernel(x_ref, a_ref, nm_ref, p_ref, gattn_ref, wo_ref, gpost_ref, gffn_ref,
                 wg_ref, wu_ref, wd_ref, gffpost_ref, wpp_ref, gple_ref, wpg_ref, o_ref):
    tm = x_ref.shape[0]
    halves = (slice(0, tm // 2), slice(tm // 2, tm))

    def swiglu(hn, cols):
        gate = jnp.dot(hn, wg_ref[:, cols], preferred_element_type=F32)
        up = jnp.dot(hn, wu_ref[:, cols], preferred_element_type=F32)
        act = (gate * jax.nn.sigmoid(gate) * up).astype(BF16)
        return jnp.dot(act, wd_ref[cols, :], preferred_element_type=F32)

    na = _rms(a_ref[...], gattn_ref[...]).astype(BF16)
    mixes = [jnp.dot(na[rows], wo_ref[:D_ATTN, :], preferred_element_type=F32)
             + jnp.dot(nm_ref[rows, :], wo_ref[D_ATTN:, :], preferred_element_type=F32)
             for rows in halves]
    hs, hns, ffs = [], [], []
    for rows, mix in zip(halves, mixes):
        h = x_ref[rows, :] + _rms(mix, gpost_ref[...])
        hn = _rms(h, gffn_ref[...]).astype(BF16)
        hs.append(h)
        hns.append(hn)
        ffs.append(swiglu(hn, slice(0, FF_CHUNK)))
    hn = jnp.concatenate(hns, axis=0)
    ff = jnp.concatenate(ffs, axis=0)
    for c0 in range(FF_CHUNK, D_FF, FF_CHUNK):
        ff = ff + swiglu(hn, slice(c0, c0 + FF_CHUNK))

    e = _rms(jnp.dot(p_ref[...].astype(BF16), wpp_ref[...], preferred_element_type=F32),
             gple_ref[...])
    for rows, h in zip(halves, hs):
        h = h + _rms(ff[rows], gffpost_ref[...])
        gate = jax.nn.sigmoid(jnp.dot(h.astype(BF16), wpg_ref[...], preferred_element_type=F32))
        o_ref[rows, :] = h + gate * e[rows]


def _tail_call(x2, a2, nm2, p2, gattn, wo, gpost, gffn, wg, wu, wd, gffpost, wpp, gple, wpg):
    t = x2.shape[0]
    tm = TAIL_ROWS
    rows = lambda width: pl.BlockSpec((tm, width), lambda i: (i, 0))
    const = lambda *shape: pl.BlockSpec(shape, lambda i: (0,) * len(shape),
                                        pipeline_mode=pl.Buffered(1))
    return pl.pallas_call(
        _tail_kernel,
        grid=(t // tm,),
        in_specs=[
            rows(D_MODEL), rows(D_ATTN), rows(D_POOL), rows(D_PLE),
            const(1, D_ATTN), const(D_MODEL, D_MODEL), const(1, D_MODEL), const(1, D_MODEL),
            const(D_MODEL, D_FF), const(D_MODEL, D_FF), const(D_FF, D_MODEL), const(1, D_MODEL),
            const(D_PLE, D_MODEL), const(1, D_MODEL), const(D_MODEL, D_MODEL),
        ],
        out_specs=rows(D_MODEL),
        out_shape=jax.ShapeDtypeStruct((t, D_MODEL), F32),
        compiler_params=pltpu.CompilerParams(
            dimension_semantics=("arbitrary",),
            vmem_limit_bytes=VMEM_LIMIT_BYTES),
        name="tail",
    )(x2, a2, nm2, p2, gattn, wo, gpost, gffn, wg, wu, wd, gffpost, wpp, gple, wpg)


def _head_sum_matrix():
    hsum = np.zeros((2 * D_ATTN, LANES), np.float32)
    for d in range(2 * D_ATTN):
        hsum[d, d // HEAD_DIM] = 1.0
    return jnp.asarray(hsum, BF16)


def kernel(x, p, g_mix_pre, w_in, b_forget, g_attn_grp, g_pool_grp, w_pool, pool_scale, w_out,
           g_mix_post, g_ffn_pre, w_ffn_gate, w_ffn_up, w_ffn_down, g_ffn_post, w_ple_proj,
           g_ple, w_ple_gate):
    b, s, _ = x.shape
    depth = w_in.shape[0]
    assert s % PROJ_ROWS == 0 and PROJ_ROWS % ATTN_Q_ROWS == 0 and (b * s) % TAIL_ROWS == 0
    assert (s // ATTN_Q_ROWS * ATTN_HEADS) % ITEMS_PER_TRIP == 0
    hsum = _head_sum_matrix()
    row = lambda v: v.reshape(1, -1).astype(F32)
    h = x
    for i in range(depth):
        w_t = jnp.swapaxes(w_in[i], 0, 1)
        o_f = 3 * D_ATTN
        o_u = o_f + ATTN_HEADS
        wf_t = w_t[o_f:o_u]
        wfu_t = jnp.concatenate(
            [wf_t, wf_t, wf_t, jnp.zeros((LANES - PIECE_LANES, D_MODEL), F32), w_t[o_u:]],
            axis=0).astype(BF16)
        bf = b_forget[i].astype(F32)
        bf3 = jnp.concatenate([bf, bf, bf, jnp.zeros((LANES - PIECE_LANES,), F32)])
        wp = w_pool[i].astype(BF16)
        zero = jnp.zeros((POOL_CH, POOL_CH), BF16)
        wpool2 = jnp.stack([jnp.block([[wp[2 * j], zero], [zero, wp[2 * j + 1]]])
                            for j in range(POOL_GROUPS // 2)])
        q, k, v, nm, stats = _proj_call(
            h, row(g_mix_pre[i]), wfu_t, w_t, bf3.reshape(1, LANES), hsum,
            wpool2, row(pool_scale[i]), row(g_pool_grp[i]))
        a, (wo, wg, wu, wd, wpp, wpg) = _attn_call(
            q, k, v, stats, [w_out[i], w_ffn_gate[i], w_ffn_up[i], w_ffn_down[i],
                             w_ple_proj[i], w_ple_gate[i]])
        t = b * s
        h = _tail_call(
            h.reshape(t, D_MODEL), a.reshape(t, D_ATTN), nm.reshape(t, D_POOL),
            p[i].reshape(t, D_PLE), row(g_attn_grp[i]), wo, row(g_mix_post[i]),
            row(g_ffn_pre[i]), wg, wu, wd, row(g_ffn_post[i]), wpp, row(g_ple[i]), wpg,
        ).reshape(b, s, D_MODEL)
    return h
```
